```python
import math
import jax, jax.numpy as jnp
from jax import lax
import numpy as np

D_MODEL = 2048
BATCH = 1
SEQ = 8192
DEPTH = 2
DEC_BATCH = 32
DEC_SEQ = 8
PAST_LEN = 8192
PAGE_SIZE = 128

HEAD_DIM = 64
ATTN_W = D_MODEL // 2
C_CONV = D_MODEL - ATTN_W
MIX_W = C_CONV + ATTN_W
N_HEADS = ATTN_W // HEAD_DIM
KV_HEADS = N_HEADS // 4
GQA = N_HEADS // KV_HEADS
KV_W = KV_HEADS * HEAD_DIM
ROT_DIM = HEAD_DIM // 4
ROPE_THETA = 500000.0
CONV_WIDTH = 31
CMP_LEN = 32
CMP_STRIDE = 16
SEL_BLOCK = 64
SEL_TOPK = 16
WINDOW = 512
Q_BLOCK = 128
PLE_DIM = 256
EPS = 1e-6
NEG = -1e30
BIG = 1e9
N_IN = 3 * C_CONV + 2 * ATTN_W + 6 * KV_W + 3 * N_HEADS

kernel_name = "hybrid_conformer_nsa_decoder_step"


def rms_norm(x, g):
    xf = x.astype(jnp.float32)
    y = xf * lax.rsqrt(jnp.mean(xf * xf, axis=-1, keepdims=True) + EPS)
    return (y * g.astype(jnp.float32)).astype(x.dtype)


def layer_norm(x, g, b):
    xf = x.astype(jnp.float32)
    mu = jnp.mean(xf, axis=-1, keepdims=True)
    xc = xf - mu
    y = xc * lax.rsqrt(jnp.mean(xc * xc, axis=-1, keepdims=True) + EPS)
    return (y * g.astype(jnp.float32) + b.astype(jnp.float32)).astype(x.dtype)


def rope(x, pos):
    half = ROT_DIM // 2
    inv = ROPE_THETA ** (-2.0 * jnp.arange(half, dtype=jnp.float32) / ROT_DIM)
    ang = pos.astype(jnp.float32)[:, None] * inv[None, :]
    cos = jnp.cos(ang)[None, :, None, :]
    sin = jnp.sin(ang)[None, :, None, :]
    xr = x[..., :ROT_DIM].astype(jnp.float32)
    x1, x2 = xr[..., :half], xr[..., half:]
    rot = jnp.concatenate([x1 * cos - x2 * sin, x2 * cos + x1 * sin], axis=-1).astype(x.dtype)
    return jnp.concatenate([rot, x[..., ROT_DIM:]], axis=-1)


def split_proj(u):
    sizes = (C_CONV, C_CONV, C_CONV, ATTN_W) + (KV_W,) * 6 + (3 * N_HEADS, ATTN_W)
    cuts, acc = [], 0
    for s in sizes[:-1]:
        acc += s
        cuts.append(acc)
    return jnp.split(u, cuts, axis=-1)


def causal_dwconv(u_ext, w, b):
    out = lax.conv_general_dilated(u_ext, w[:, None, :].astype(u_ext.dtype), window_strides=(1,), padding='VALID',
                                   dimension_numbers=('NWC', 'WIO', 'NWC'), feature_group_count=C_CONV)
    return out + b


def compress(k, w_c):
    B, L = k.shape[:2]
    n_chunk = -(-L // CMP_STRIDE)
    k = jnp.pad(k, ((0, 0), (0, n_chunk * CMP_STRIDE - L), (0, 0), (0, 0)))
    ch = k.reshape(B, n_chunk, CMP_STRIDE, KV_HEADS, HEAD_DIM)
    r = CMP_LEN // CMP_STRIDE
    nc = n_chunk - r + 1
    blk = jnp.concatenate([ch[:, i:i + nc] for i in range(r)], axis=2)
    out = jnp.einsum('bnlgd,lde->bnge', blk, w_c.reshape(CMP_LEN, HEAD_DIM, HEAD_DIM).astype(k.dtype))
    ends = jnp.arange(nc) * CMP_STRIDE + CMP_LEN - 1
    return out, ends


def to_blocks(k):
    B, L = k.shape[:2]
    nsb = -(-L // SEL_BLOCK)
    k = jnp.pad(k, ((0, 0), (0, nsb * SEL_BLOCK - L), (0, 0), (0, 0)))
    return k.reshape(B, nsb, SEL_BLOCK, KV_HEADS, HEAD_DIM).transpose(0, 3, 1, 2, 4)


def cmp_to_sel(nc, nsb):
    n = jnp.arange(nc)[:, None] * CMP_STRIDE
    j = jnp.arange(nsb)[None, :]
    return ((n < (j + 1) * SEL_BLOCK) & (n + CMP_LEN > j * SEL_BLOCK)).astype(jnp.float32)


def nsa_core(q, q_pos, ck, cv, c_end, skb, svb, wk, wv, w_pos, gates):
    B, T = q.shape[:2]
    scale = HEAD_DIM ** -0.5
    qg = q.reshape(B, T, KV_HEADS, GQA, HEAD_DIM)
    t = q_pos[:, None]
    s = jnp.einsum('btgqd,bngd->btgqn', qg, ck).astype(jnp.float32) * scale
    m = (c_end[None, :] <= t)[None, :, None, None, :]
    p = jnp.where(m, jax.nn.softmax(jnp.where(m, s, NEG), axis=-1), 0.0)
    o_c = jnp.einsum('btgqn,bngd->btgqd', p.astype(cv.dtype), cv)
    nsb = skb.shape[2]
    imp = jnp.einsum('btgqn,nj->btgj', p, cmp_to_sel(ck.shape[1], nsb))
    j = jnp.arange(nsb)[None, :]
    cur = t // SEL_BLOCK
    forced = ((j == 0) | (j == cur) | (j == cur - 1))[None, :, None, :]
    valid = (j * SEL_BLOCK <= t)[None, :, None, :]
    score = jnp.where(valid, jnp.where(forced, BIG, imp), NEG)
    _, idx = lax.top_k(score, min(SEL_TOPK, nsb))
    bi = jnp.arange(B)[:, None, None, None]
    gi = jnp.arange(KV_HEADS)[None, None, :, None]
    ks = skb[bi, gi, idx]
    vs = svb[bi, gi, idx]
    kpos = idx[..., None] * SEL_BLOCK + jnp.arange(SEL_BLOCK)
    m2 = (kpos <= q_pos[None, :, None, None, None])[:, :, :, None]
    s2 = jnp.einsum('btgqd,btgksd->btgqks', qg, ks).astype(jnp.float32) * scale
    s2 = jnp.where(m2, s2, NEG)
    sh = s2.shape
    p2 = jax.nn.softmax(s2.reshape(sh[:4] + (sh[4] * sh[5],)), axis=-1).reshape(sh)
    o_s = jnp.einsum('btgqks,btgksd->btgqd', p2.astype(vs.dtype), vs)
    dlt = t - w_pos[None, :]
    m3 = ((dlt >= 0) & (dlt <= WINDOW) & (w_pos[None, :] >= 0))[None, :, None, None, :]
    s3 = jnp.einsum('btgqd,bkgd->btgqk', qg, wk).astype(jnp.float32) * scale
    p3 = jax.nn.softmax(jnp.where(m3, s3, NEG), axis=-1)
    o_w = jnp.einsum('btgqk,bkgd->btgqd', p3.astype(wv.dtype), wv)
    g = jax.nn.sigmoid(gates.astype(jnp.float32)).reshape(B, T, 3, KV_HEADS, GQA)[..., None]
    o = g[:, :, 0] * o_c + g[:, :, 1] * o_s + g[:, :, 2] * o_w
    return o.reshape(B, T, ATTN_W).astype(q.dtype)


def nsa_prompt(q, kc, vc, ks, vs, kw, vw, gates, w_ck, w_cv):
    B, S = q.shape[:2]
    ck, c_end = compress(kc, w_ck)
    cv, _ = compress(vc, w_cv)
    skb, svb = to_blocks(ks), to_blocks(vs)
    kw_p = jnp.pad(kw, ((0, 0), (WINDOW, 0), (0, 0), (0, 0)))
    vw_p = jnp.pad(vw, ((0, 0), (WINDOW, 0), (0, 0), (0, 0)))

    def one(i):
        q0 = i * Q_BLOCK
        qb = lax.dynamic_slice_in_dim(q, q0, Q_BLOCK, axis=1)
        gb = lax.dynamic_slice_in_dim(gates, q0, Q_BLOCK, axis=1)
        wkb = lax.dynamic_slice_in_dim(kw_p, q0, WINDOW + Q_BLOCK, axis=1)
        wvb = lax.dynamic_slice_in_dim(vw_p, q0, WINDOW + Q_BLOCK, axis=1)
        qp = q0 + jnp.arange(Q_BLOCK)
        wp = q0 - WINDOW + jnp.arange(WINDOW + Q_BLOCK)
        return nsa_core(qb, qp, ck, cv, c_end, skb, svb, wkb, wvb, wp, gb)

    out = lax.map(one, jnp.arange(S // Q_BLOCK))
    return out.transpose(1, 0, 2, 3).reshape(B, S, ATTN_W)


def gather_pages(cache, page_table):
    g = cache[page_table]
    return g.reshape(g.shape[0], g.shape[1] * g.shape[2], KV_HEADS, HEAD_DIM)


def layer(h, pe, pos, conv_hist, past, norm_g, w_in, conv_w, conv_b, ln_g, ln_b, w_ck, w_cv, w_out, w_ple, w_ple_gate):
    B, T = h.shape[:2]
    hn = rms_norm(h, norm_g)
    a, b, zc, q, kc, vc, ks, vs, kw, vw, gt, za = split_proj(hn @ w_in)
    u = a * jax.nn.sigmoid(b)
    u_ext = jnp.concatenate([conv_hist, u], axis=1)
    yc = jax.nn.silu(layer_norm(causal_dwconv(u_ext, conv_w, conv_b), ln_g, ln_b)) * jax.nn.silu(zc)
    new_conv = u_ext[:, -(CONV_WIDTH - 1):]
    q = rope(q.reshape(B, T, N_HEADS, HEAD_DIM), pos)
    kc = rope(kc.reshape(B, T, KV_HEADS, HEAD_DIM), pos)
    ks = rope(ks.reshape(B, T, KV_HEADS, HEAD_DIM), pos)
    kw = rope(kw.reshape(B, T, KV_HEADS, HEAD_DIM), pos)
    vc = vc.reshape(B, T, KV_HEADS, HEAD_DIM)
    vs = vs.reshape(B, T, KV_HEADS, HEAD_DIM)
    vw = vw.reshape(B, T, KV_HEADS, HEAD_DIM)
    if past is None:
        ya = nsa_prompt(q, kc, vc, ks, vs, kw, vw, gt, w_ck, w_cv)
        wb = min(WINDOW, T)
        new_kw, new_vw = kw[:, T - wb:], vw[:, T - wb:]
    else:
        pkc, pvc, pks, pvs, bkw, bvw = past
        ck, c_end = compress(jnp.concatenate([pkc, kc], axis=1), w_ck)
        cv, _ = compress(jnp.concatenate([pvc, vc], axis=1), w_cv)
        skb = to_blocks(jnp.concatenate([pks, ks], axis=1))
        svb = to_blocks(jnp.concatenate([pvs, vs], axis=1))
        wb = bkw.shape[1]
        wk_all = jnp.concatenate([bkw, kw], axis=1)
        wv_all = jnp.concatenate([bvw, vw], axis=1)
        w_pos = pos[0] - wb + jnp.arange(wb + T)
        ya = nsa_core(q, pos, ck, cv, c_end, skb, svb, wk_all, wv_all, w_pos, gt)
        new_kw, new_vw = wk_all[:, -wb:], wv_all[:, -wb:]
    ya = ya * jax.nn.silu(za)
    h = h + jnp.concatenate([yc, ya], axis=-1) @ w_out
    h = h + jax.nn.sigmoid(h @ w_ple_gate) * (pe @ w_ple)
    return h, (kc, vc, ks, vs, new_kw, new_vw, new_conv)


def setup_inputs(seed: int = 0) -> dict:
    key = jax.random.key(seed)
    k = jax.random.split(key, 26)
    n_pages = PAST_LEN // PAGE_SIZE
    n_used = DEC_BATCH * n_pages
    n_pool = n_used + max(1, n_used // 4)
    wb = min(WINDOW, PAST_LEN)

    def nrm(kk, shape, s=1.0):
        return s * jax.random.normal(kk, shape, jnp.float32)

    pool_shape = (DEPTH, n_pool, PAGE_SIZE, KV_HEADS, HEAD_DIM)
    page_table = jax.random.permutation(k[10], n_pool)[:n_used].reshape(DEC_BATCH, n_pages).astype(jnp.int32)
    return {
        "x_prompt": nrm(k[0], (BATCH, SEQ, D_MODEL)),
        "x_sample": nrm(k[1], (DEC_BATCH, DEC_SEQ, D_MODEL)),
        "cache_cmp_k": nrm(k[2], pool_shape),
        "cache_cmp_v": nrm(k[3], pool_shape),
        "cache_slc_k": nrm(k[4], pool_shape),
        "cache_slc_v": nrm(k[5], pool_shape),
        "cache_win_k": nrm(k[6], (DEPTH, DEC_BATCH, wb, KV_HEADS, HEAD_DIM)),
        "cache_win_v": nrm(k[7], (DEPTH, DEC_BATCH, wb, KV_HEADS, HEAD_DIM)),
        "state_conv": nrm(k[8], (DEPTH, DEC_BATCH, CONV_WIDTH - 1, C_CONV), 0.5),
        "page_table": page_table,
        "p_prompt": nrm(k[11], (DEPTH, BATCH, SEQ, PLE_DIM)),
        "p_sample": nrm(k[12], (DEPTH, DEC_BATCH, DEC_SEQ, PLE_DIM)),
        "norm_g": 1.0 + nrm(k[13], (DEPTH, D_MODEL), 0.01),
        "w_in": nrm(k[14], (DEPTH, D_MODEL, N_IN), D_MODEL ** -0.5),
        "conv_w": nrm(k[15], (DEPTH, CONV_WIDTH, C_CONV), CONV_WIDTH ** -0.5),
        "conv_b": nrm(k[16], (DEPTH, C_CONV), 0.01),
        "conv_ln_g": 1.0 + nrm(k[17], (DEPTH, C_CONV), 0.01),
        "conv_ln_b": nrm(k[18], (DEPTH, C_CONV), 0.01),
        "w_cmp_k": nrm(k[19], (DEPTH, CMP_LEN * HEAD_DIM, HEAD_DIM), (CMP_LEN * HEAD_DIM) ** -0.5),
        "w_cmp_v": nrm(k[20], (DEPTH, CMP_LEN * HEAD_DIM, HEAD_DIM), (CMP_LEN * HEAD_DIM) ** -0.5),
        "w_out": nrm(k[21], (DEPTH, MIX_W, D_MODEL), MIX_W ** -0.5),
        "w_ple": nrm(k[22], (DEPTH, PLE_DIM, D_MODEL), PLE_DIM ** -0.5),
        "w_ple_gate": nrm(k[23], (DEPTH, D_MODEL, D_MODEL), D_MODEL ** -0.5),
        "final_g": 1.0 + nrm(k[24], (D_MODEL,), 0.01),
    }


def reference(x_prompt, x_sample, cache_cmp_k, cache_cmp_v, cache_slc_k, cache_slc_v, cache_win_k, cache_win_v,
              state_conv, page_table, p_prompt, p_sample, norm_g, w_in, conv_w, conv_b, conv_ln_g, conv_ln_b,
              w_cmp_k, w_cmp_v, w_out, w_ple, w_ple_gate, final_g):
    hp, hs = x_prompt, x_sample
    pos_p = jnp.arange(x_prompt.shape[1])
    pos_s = PAST_LEN + jnp.arange(x_sample.shape[1])
    zero_hist = jnp.zeros((x_prompt.shape[0], CONV_WIDTH - 1, C_CONV), x_prompt.dtype)
    st_p, st_s = [], []
    for i in range(DEPTH):
        prm = (norm_g[i], w_in[i], conv_w[i], conv_b[i], conv_ln_g[i], conv_ln_b[i],
               w_cmp_k[i], w_cmp_v[i], w_out[i], w_ple[i], w_ple_gate[i])
        hp, sp = layer(hp, p_prompt[i], pos_p, zero_hist, None, *prm)
        past = (gather_pages(cache_cmp_k[i], page_table), gather_pages(cache_cmp_v[i], page_table),
                gather_pages(cache_slc_k[i], page_table), gather_pages(cache_slc_v[i], page_table),
                cache_win_k[i], cache_win_v[i])
        hs, ss = layer(hs, p_sample[i], pos_s, state_conv[i], past, *prm)
        st_p.append(sp)
        st_s.append(ss)
    y_prompt = rms_norm(hp, final_g)
    y_sample = rms_norm(hs, final_g)
    pc_k, pc_v, ps_k, ps_v, pw_k, pw_v, p_conv = [jnp.stack([s[n] for s in st_p]) for n in range(7)]
    sc_k, sc_v, ss_k, ss_v, sw_k, sw_v, s_conv = [jnp.stack([s[n] for s in st_s]) for n in range(7)]
    return (y_prompt, y_sample, pc_k, pc_v, ps_k, ps_v, pw_k, pw_v, p_conv,
            sc_k, sc_v, ss_k, ss_v, sw_k, sw_v, s_conv)
```

```python
import functools

import jax
import jax.numpy as jnp
from jax import lax
from jax.experimental import pallas as pl
from jax.experimental.pallas import tpu as pltpu

F32 = jnp.float32
BF16 = jnp.bfloat16

D_MODEL = 2048
SEQ = 8192
DEPTH = 2
DEC_BATCH = 32
DEC_SEQ = 8
PAST_LEN = 8192
PAGE_SIZE = 128
HEAD_DIM = 64
ATTN_W = D_MODEL // 2
C_CONV = D_MODEL - ATTN_W
N_HEADS = ATTN_W // HEAD_DIM
KV_HEADS = N_HEADS // 4
GQA = N_HEADS // KV_HEADS
KV_W = KV_HEADS * HEAD_DIM
ROT_DIM = HEAD_DIM // 4
ROPE_THETA = 500000.0
CONV_WIDTH = 31
CMP_LEN = 32
CMP_STRIDE = 16
SEL_BLOCK = 64
SEL_TOPK = 16
WINDOW = 512
Q_BLOCK = 128
PLE_DIM = 256
EPS = 1e-6
NEG = -1e30
BIG = 1e9

N_SAMPLE = DEC_BATCH * DEC_SEQ
N_ROWS = SEQ + N_SAMPLE
N_PAGES = PAST_LEN // PAGE_SIZE
N_CHUNK = PAST_LEN // CMP_STRIDE
N_CMP = N_CHUNK
CHUNK_W = CMP_STRIDE * KV_W
PAGE_CHUNKS = PAGE_SIZE // CMP_STRIDE
NSB_P = SEQ // SEL_BLOCK
NSB_S = -(-(PAST_LEN + DEC_SEQ) // SEL_BLOCK)
NSB_S_PAD = 256
BELOW_NEG = -3e38
SCALE = HEAD_DIM ** -0.5

LANES = 128
SUBLANES = 8
VMEM_LIMIT = 56 * 1024 * 1024

A_COLS = 4 * C_CONV
B_COLS = ATTN_W + 6 * KV_W
B_ROPE_COLS = ATTN_W + 3 * KV_W
ROW_TILE = N_ROWS // 8
A_TN = 512
B_TN = 256
SLC_KT = 512


def _dot(a, b):
    return jnp.dot(a, b, preferred_element_type=F32)


def _dot_nt(a, b):
    return lax.dot_general(a, b, (((1,), (1,)), ((), ())), preferred_element_type=F32)


def _lane_tile(x, n):
    return x if n == 1 else jnp.concatenate([x] * n, axis=1)


def _sigmoid(x):
    return 1.0 / (1.0 + jnp.exp(-x))


def _inproj_kernel(*refs, rope_tiles, extras):
    if extras:
        x_ref, g_ref, w_ref, wg_ref, cos_ref, sa_ref, sb_ref, u_ref, ub_ref, gate_ref, hn_ref = refs
    else:
        x_ref, g_ref, w_ref, u_ref, hn_ref = refs
    j = pl.program_id(1)

    @pl.when(j == 0)
    def _():
        x = x_ref[...]
        r = lax.rsqrt(jnp.mean(x * x, axis=-1, keepdims=True) + EPS)
        hn = ((x * r) * g_ref[...]).astype(BF16)
        hn_ref[...] = hn
        if extras:
            gate_ref[...] = _dot(hn, wg_ref[...])

    acc = _dot(hn_ref[...], w_ref[...])
    if not extras:
        u_ref[...] = acc
        return

    @pl.when(j < rope_tiles)
    def _():
        c, sa, sb = cos_ref[...], sa_ref[...], sb_ref[...]
        for k in range(acc.shape[1] // LANES):
            cs = slice(k * LANES, (k + 1) * LANES)
            xs = acc[:, cs]
            y = xs * c + pltpu.roll(xs, LANES - ROT_DIM // 2, 1) * sa + pltpu.roll(xs, ROT_DIM // 2, 1) * sb
            u_ref[:, cs] = y
            ub_ref[:, cs] = y.astype(BF16)

    @pl.when(j >= rope_tiles)
    def _():
        u_ref[...] = acc
        ub_ref[...] = acc.astype(BF16)


def _inproj(x, g, w, extras=None):
    n_cols = w.shape[1]
    tn = B_TN if extras else A_TN
    grid = (N_ROWS // ROW_TILE, n_cols // tn)
    row = lambda i, j: (i, 0)
    in_specs = [pl.BlockSpec((ROW_TILE, D_MODEL), row),
                pl.BlockSpec((1, D_MODEL), lambda i, j: (0, 0)),
                pl.BlockSpec((D_MODEL, tn), lambda i, j: (0, j))]
    out_specs = [pl.BlockSpec((ROW_TILE, tn), lambda i, j: (i, j))]
    out_shape = [jax.ShapeDtypeStruct((N_ROWS, n_cols), F32)]
    args = [x, g, w]
    if extras:
        wg, cos, sa, sb = extras
        in_specs += [pl.BlockSpec((D_MODEL, LANES), lambda i, j: (0, 0))] + [pl.BlockSpec((ROW_TILE, LANES), row)] * 3
        out_specs += [pl.BlockSpec((ROW_TILE, tn), lambda i, j: (i, j)), pl.BlockSpec((ROW_TILE, LANES), row)]
        out_shape += [jax.ShapeDtypeStruct((N_ROWS, n_cols), BF16), jax.ShapeDtypeStruct((N_ROWS, LANES), F32)]
        args += [wg, cos, sa, sb]
    return pl.pallas_call(
        functools.partial(_inproj_kernel, rope_tiles=B_ROPE_COLS // B_TN, extras=bool(extras)),
        grid=grid, in_specs=in_specs, out_specs=out_specs, out_shape=out_shape,
        scratch_shapes=[pltpu.VMEM((ROW_TILE, D_MODEL), BF16)],
        compiler_params=pltpu.CompilerParams(dimension_semantics=("parallel", "arbitrary"),
                                             vmem_limit_bytes=VMEM_LIMIT),
        name="inproj_b" if extras else "inproj_a",
    )(*args)


HIST_PAD = 32
HIST_OFF = HIST_PAD - (CONV_WIDTH - 1)


def _conv_kernel(a_ref, b_ref, zc_ref, hist_ref, w_ref, cb_ref, lg_ref, lb_ref, yc_ref, nc_ref, ext_ref, y_ref, *, tm):
    t = pl.program_id(1)

    @pl.when(t == 0)
    def _():
        ext_ref[0:HIST_PAD, :] = hist_ref[0]

    ext_ref[HIST_PAD:HIST_PAD + tm, :] = a_ref[...] * _sigmoid(b_ref[...])
    for c in range(C_CONV // LANES):
        cs = slice(c * LANES, (c + 1) * LANES)
        acc = jnp.zeros((tm, LANES), F32) + cb_ref[:, cs]
        for k in range(CONV_WIDTH):
            acc = acc + ext_ref[HIST_OFF + k:HIST_OFF + k + tm, cs] * w_ref[k:k + 1, cs]
        y_ref[:, cs] = acc
    y = y_ref[...]
    mu = jnp.mean(y, axis=-1, keepdims=True)
    yc = y - mu
    yn = yc * lax.rsqrt(jnp.mean(yc * yc, axis=-1, keepdims=True) + EPS) * lg_ref[...] + lb_ref[...]
    zc = zc_ref[...]
    yc_ref[...] = ((yn * _sigmoid(yn)) * (zc * _sigmoid(zc))).astype(yc_ref.dtype)
    tail = ext_ref[tm:tm + HIST_PAD, :]
    ext_ref[0:HIST_PAD, :] = tail

    @pl.when(t == pl.num_programs(1) - 1)
    def _():
        nc_ref[0] = tail


def _conv(ua, hist, w, cb, lg, lb, *, row0, batch, t_len, tm):
    nt = t_len // tm
    blk0 = row0 // tm
    col = lambda c: (lambda b, t: (blk0 + b * nt + t, c))
    vec = pl.BlockSpec((1, C_CONV), lambda b, t: (0, 0))
    return pl.pallas_call(
        functools.partial(_conv_kernel, tm=tm),
        grid=(batch, nt),
        in_specs=[pl.BlockSpec((tm, C_CONV), col(0)), pl.BlockSpec((tm, C_CONV), col(1)),
                  pl.BlockSpec((tm, C_CONV), col(2)),
                  pl.BlockSpec((1, HIST_PAD, C_CONV), lambda b, t: (b, 0, 0)),
                  pl.BlockSpec((CONV_WIDTH, C_CONV), lambda b, t: (0, 0)), vec, vec, vec],
        out_specs=[pl.BlockSpec((tm, C_CONV), lambda b, t: (b * nt + t, 0)),
                   pl.BlockSpec((1, HIST_PAD, C_CONV), lambda b, t: (b, 0, 0))],
        out_shape=[jax.ShapeDtypeStruct((batch * t_len, C_CONV), BF16 if tm % 16 == 0 else F32),
                   jax.ShapeDtypeStruct((batch, HIST_PAD, C_CONV), F32)],
        scratch_shapes=[pltpu.VMEM((HIST_PAD + tm, C_CONV), F32), pltpu.VMEM((tm, C_CONV), F32)],
        compiler_params=pltpu.CompilerParams(dimension_semantics=("parallel", "arbitrary"),
                                             vmem_limit_bytes=VMEM_LIMIT),
        name="conv_group",
    )(ua, ua, ua, hist, w, cb, lg, lb)


def _compress_math(x, next_row, wab):
    z = _dot(x, wab)
    zb = pltpu.roll(z[:, KV_W:], N_CHUNK - 1, 0)
    rows = lax.broadcasted_iota(jnp.int32, zb.shape, 0)
    zb = jnp.where(rows == N_CHUNK - 1, next_row, zb)
    return z[:, :KV_W] + zb


def _pcompress_kernel(x_ref, wab_ref, o_ref):
    o_ref[...] = _compress_math(x_ref[...].astype(BF16), jnp.zeros((1, KV_W), F32), wab_ref[...]).astype(o_ref.dtype)


def _pcompress(x, wab):
    return pl.pallas_call(
        _pcompress_kernel,
        out_shape=jax.ShapeDtypeStruct((N_CMP, KV_W), BF16),
        compiler_params=pltpu.CompilerParams(vmem_limit_bytes=VMEM_LIMIT),
        name="compress_prompt",
    )(x, wab)


def _page_copy(pt_ref, cache_ref, buf, sem, layer, b, p, slot, rows):
    return pltpu.make_async_copy(cache_ref.at[layer, pt_ref[b, p]], buf.at[slot, pl.ds(p * rows, rows)], sem.at[slot])


def _scompress_kernel(pt_ref, cache_ref, xnew_ref, wab_ref, o_ref, xbuf, sem, *, layer):
    b = pl.program_id(0)
    slot = b % 2

    def copies(bb, sl, go):
        def body(p, carry):
            cp = _page_copy(pt_ref, cache_ref, xbuf, sem, layer, bb, p, sl, PAGE_CHUNKS)
            cp.start() if go else cp.wait()
            return carry
        lax.fori_loop(0, N_PAGES, body, 0)

    @pl.when(b == 0)
    def _():
        copies(0, 0, True)

    @pl.when(b + 1 < pl.num_programs(0))
    def _():
        copies(b + 1, 1 - slot, True)

    copies(b, slot, False)
    wab = wab_ref[...]
    zn = _dot(xnew_ref[0].astype(BF16), wab)
    o_ref[0] = _compress_math(xbuf[slot].astype(BF16), zn[0:1, KV_W:], wab).astype(o_ref.dtype)


def _scompress(page_table, cache, xnew, wab, layer):
    return pl.pallas_call(
        functools.partial(_scompress_kernel, layer=layer),
        grid_spec=pltpu.PrefetchScalarGridSpec(
            num_scalar_prefetch=1, grid=(DEC_BATCH,),
            in_specs=[pl.BlockSpec(memory_space=pl.ANY),
                      pl.BlockSpec((1, SUBLANES, CHUNK_W), lambda b, pt: (b, 0, 0)),
                      pl.BlockSpec((CHUNK_W, 2 * KV_W), lambda b, pt: (0, 0))],
            out_specs=pl.BlockSpec((1, N_CMP, KV_W), lambda b, pt: (b, 0, 0)),
            scratch_shapes=[pltpu.VMEM((2, N_CHUNK, CHUNK_W), F32), pltpu.SemaphoreType.DMA((2,))]),
        out_shape=jax.ShapeDtypeStruct((DEC_BATCH, N_CMP, KV_W), BF16),
        compiler_params=pltpu.CompilerParams(dimension_semantics=("arbitrary",), vmem_limit_bytes=VMEM_LIMIT),
        name="compress_sample",
    )(page_table, cache, xnew, wab)


def _cmp_branch(q, tcol, ck, cv, mmat):
    s = _dot_nt(q, ck)
    n = lax.broadcasted_iota(jnp.int32, s.shape, 1)
    vis = (n * CMP_STRIDE + (CMP_LEN - 1)) <= tcol
    sm = jnp.where(vis, s, NEG)
    e = jnp.where(vis, jnp.exp(sm - jnp.max(sm, axis=1, keepdims=True)), 0.0)
    den = jnp.sum(e, axis=1, keepdims=True)
    p = (e / jnp.where(den > 0.0, den, 1.0)).astype(BF16)
    return _dot(p, cv), _dot(p, mmat)


def _select_bias(imp, tcol, nsb):
    j = lax.broadcasted_iota(jnp.int32, imp.shape, 1)
    cur = lax.shift_right_arithmetic(tcol, 6)
    forced = (j == 0) | (j == cur) | (j == cur - 1)
    score = jnp.where(j * SEL_BLOCK <= tcol, jnp.where(forced, BIG, imp), NEG)
    score = jnp.where(j < nsb, score, BELOW_NEG)
    lane = j.astype(F32)
    bias = jnp.full(imp.shape, NEG, F32)
    for _ in range(SEL_TOPK):
        mx = jnp.max(score, axis=1, keepdims=True)
        idx = jnp.min(jnp.where(score == mx, lane, 1e9), axis=1, keepdims=True)
        pick = lane == idx
        bias = jnp.where(pick, 0.0, bias)
        score = jnp.where(pick, BELOW_NEG, score)
    return bias.astype(BF16)


def _block_onehot(nb, kt, first_block):
    j = lax.broadcasted_iota(jnp.int32, (nb, kt), 0)
    col = lax.broadcasted_iota(jnp.int32, (nb, kt), 1)
    return jnp.where(j - lax.shift_right_arithmetic(col, 6) == first_block, 1.0, 0.0).astype(BF16)


def _online_update(s, v, m_ref, l_ref, acc_ref, rows):
    m_prev, l_prev = m_ref[rows, :], l_ref[rows, :]
    m_next = jnp.maximum(m_prev, jnp.max(s, axis=1, keepdims=True))
    alpha = jnp.exp(m_prev - m_next)
    p = jnp.exp(s - _lane_tile(m_next, s.shape[1] // LANES))
    l_ref[rows, :] = alpha * l_prev + jnp.sum(p, axis=1, keepdims=True)
    m_ref[rows, :] = m_next
    acc_ref[rows, :] = acc_ref[rows, :] * _lane_tile(alpha, KV_W // LANES) + _dot(p.astype(BF16), v)


def _window_mask(tcol, kpos):
    return lax.bitcast_convert_type(tcol - kpos, jnp.uint32) <= jnp.uint32(WINDOW)


GROUP_ROWS = GQA * Q_BLOCK


def _pattn_kernel(q_ref, ck_ref, cv_ref, mm_ref, ks_ref, vs_ref, kw_ref, vw_ref, gt_ref, za_ref, o_ref,
                  qbd_ref, a_ref, oc_ref, acc_ref, m_ref, l_ref):
    i = pl.program_id(0)
    q0 = i * Q_BLOCK
    T, R = Q_BLOCK, GROUP_ROWS
    half = lax.shift_right_arithmetic(lax.broadcasted_iota(jnp.int32, (T, LANES), 1), 6)
    tcol = q0 + (lax.broadcasted_iota(jnp.int32, (R, 1), 0) & (T - 1))

    zeros = jnp.zeros((T, LANES), BF16)
    for hh in range(N_HEADS):
        g = hh // GQA
        pair = q_ref[:, (hh // 2) * LANES:(hh // 2 + 1) * LANES] * SCALE
        if hh % 2 != g % 2:
            pair = pltpu.roll(pair, HEAD_DIM, 1)
        x = jnp.where(half == g % 2, pair, 0.0).astype(BF16)
        qbd_ref[hh * T:(hh + 1) * T, 0:LANES] = x if g // 2 == 0 else zeros
        qbd_ref[hh * T:(hh + 1) * T, LANES:2 * LANES] = zeros if g // 2 == 0 else x

    for g in range(KV_HEADS):
        rows = slice(g * R, (g + 1) * R)
        o_c, imp_rows = _cmp_branch(qbd_ref[rows, :], tcol, ck_ref[...], cv_ref[...], mm_ref[...])
        oc_ref[rows, :] = o_c
        imp = imp_rows[0:T] + imp_rows[T:2 * T] + imp_rows[2 * T:3 * T] + imp_rows[3 * T:4 * T]
        bias = _select_bias(imp, tcol[0:T], NSB_P)
        for h in range(GQA):
            a_ref[g * R + h * T:g * R + (h + 1) * T, :] = bias

    m_ref[...] = jnp.full(m_ref.shape, NEG, F32)
    l_ref[...] = jnp.zeros(l_ref.shape, F32)
    acc_ref[...] = jnp.zeros(acc_ref.shape, F32)
    blocks_per_tile = SLC_KT // SEL_BLOCK
    kcol = lax.broadcasted_iota(jnp.int32, (1, SLC_KT), 1)

    def slc_tile(c, causal):
        k0 = pl.multiple_of(c * SLC_KT, SLC_KT)
        k_t = ks_ref[pl.ds(k0, SLC_KT), :]
        v_t = vs_ref[pl.ds(k0, SLC_KT), :]
        onehot = _block_onehot(NSB_P, SLC_KT, c * blocks_per_tile)
        for g in range(KV_HEADS):
            rows = slice(g * R, (g + 1) * R)
            s = _dot_nt(qbd_ref[rows, :], k_t) + _dot(a_ref[rows, :], onehot)
            if causal:
                s = jnp.where(k0 + kcol <= tcol, s, NEG)
            _online_update(s, v_t, m_ref, l_ref, acc_ref, rows)

    c_last = (q0 + T - 1) // SLC_KT

    def full_tile(c, carry):
        slc_tile(c, False)
        return carry

    lax.fori_loop(0, c_last, full_tile, 0)
    slc_tile(c_last, True)

    start = pl.multiple_of(jnp.maximum(q0 - WINDOW, 0), Q_BLOCK)
    kw_t = kw_ref[pl.ds(start, WINDOW + T), :]
    vw_t = vw_ref[pl.ds(start, WINDOW + T), :]
    kpos = start + lax.broadcasted_iota(jnp.int32, (1, WINDOW + T), 1)
    sig = _sigmoid(gt_ref[...])
    for g in range(KV_HEADS):
        rows = slice(g * R, (g + 1) * R)
        cc = slice((g // 2) * LANES, (g // 2 + 1) * LANES)
        s3 = jnp.where(_window_mask(tcol, kpos), _dot_nt(qbd_ref[rows, :], kw_t), NEG)
        e3 = jnp.exp(s3 - jnp.max(s3, axis=1, keepdims=True))
        o_w = _dot(e3.astype(BF16), vw_t)[:, cc] / jnp.sum(e3, axis=1, keepdims=True)
        o_s = acc_ref[rows, cc] / l_ref[rows, :]
        o_c = oc_ref[rows, cc]
        for pair in range(2):
            pieces = []
            for sub in range(2):
                h = 2 * pair + sub
                hh = g * GQA + h
                hr = slice(h * T, (h + 1) * T)
                comb = (sig[:, hh:hh + 1] * o_c[hr] + sig[:, N_HEADS + hh:N_HEADS + hh + 1] * o_s[hr]
                        + sig[:, 2 * N_HEADS + hh:2 * N_HEADS + hh + 1] * o_w[hr])
                pieces.append(comb if g % 2 == sub else pltpu.roll(comb, HEAD_DIM, 1))
            oc_cols = slice((2 * g + pair) * LANES, (2 * g + pair + 1) * LANES)
            za = za_ref[:, oc_cols]
            o_ref[:, oc_cols] = (jnp.where(half == 0, pieces[0], pieces[1]) * (za * _sigmoid(za))).astype(BF16)


def _pattn(ub, ck, cv, mmat, ubb, gates, ua):
    const = lambda r, c: (lambda i: (r, c))
    resident = lambda c: pl.BlockSpec((SEQ, KV_W), const(0, c), pipeline_mode=pl.Buffered(1))
    kv0 = ATTN_W // KV_W
    small = lambda shape: pl.BlockSpec(shape, const(0, 0))
    return pl.pallas_call(
        _pattn_kernel,
        grid=(SEQ // Q_BLOCK,),
        in_specs=[pl.BlockSpec((Q_BLOCK, ATTN_W), lambda i: (i, 0)),
                  small((N_CMP, KV_W)), small((N_CMP, KV_W)), small((N_CMP, NSB_P)),
                  resident(kv0 + 1), resident(kv0 + 4), resident(kv0 + 2), resident(kv0 + 5),
                  pl.BlockSpec((Q_BLOCK, LANES), lambda i: (i, 0)),
                  pl.BlockSpec((Q_BLOCK, ATTN_W), lambda i: (i, 3))],
        out_specs=pl.BlockSpec((Q_BLOCK, ATTN_W), lambda i: (i, 0)),
        out_shape=jax.ShapeDtypeStruct((SEQ, ATTN_W), BF16),
        scratch_shapes=[pltpu.VMEM((N_HEADS * Q_BLOCK, KV_W), BF16),
                        pltpu.VMEM((N_HEADS * Q_BLOCK, NSB_P), BF16),
                        pltpu.VMEM((N_HEADS * Q_BLOCK, KV_W), F32),
                        pltpu.VMEM((N_HEADS * Q_BLOCK, KV_W), F32),
                        pltpu.VMEM((N_HEADS * Q_BLOCK, LANES), F32),
                        pltpu.VMEM((N_HEADS * Q_BLOCK, LANES), F32)],
        compiler_params=pltpu.CompilerParams(dimension_semantics=("parallel",), vmem_limit_bytes=VMEM_LIMIT),
        name="attn_prompt",
    )(ub, ck, cv, mmat, ubb, ubb, ubb, ubb, gates, ua)


SLC_PAGES = 16
SLC_CHUNKS = N_PAGES // SLC_PAGES
SLC_CHUNK_ROWS = SLC_PAGES * PAGE_SIZE
S_ROWS = N_HEADS * DEC_SEQ
NEW_PAD = LANES


def _sattn_kernel(pt_ref, qt_ref, ck_ref, cv_ref, mm_ref, ksc_ref, vsc_ref, ksn_ref, vsn_ref,
                  kwc_ref, vwc_ref, kwn_ref, vwn_ref, gt_ref, za_ref, o_ref,
                  kbuf, vbuf, ksem, vsem, newk_ref, newv_ref, a_ref, acc_ref, m_ref, l_ref, *, layer):
    b = pl.program_id(0)
    nb = pl.num_programs(0)
    R = S_ROWS
    row = lax.broadcasted_iota(jnp.int32, (R, 1), 0)
    tcol = PAST_LEN + (row & (DEC_SEQ - 1))
    lane_group = lax.shift_right_arithmetic(lax.broadcasted_iota(jnp.int32, (R, KV_W), 1), 6)
    diag = lane_group == lax.shift_right_arithmetic(row, 5)

    def chunk_copies(bb, c, slot, go):
        def body(p, carry):
            for cache, buf, sem in ((ksc_ref, kbuf, ksem), (vsc_ref, vbuf, vsem)):
                cp = pltpu.make_async_copy(cache.at[layer, pt_ref[bb, c * SLC_PAGES + p]],
                                           buf.at[slot, pl.ds(p * PAGE_SIZE, PAGE_SIZE)], sem.at[slot])
                cp.start() if go else cp.wait()
            return carry
        lax.fori_loop(0, SLC_PAGES, body, 0)

    @pl.when(b == 0)
    def _():
        chunk_copies(0, 0, 0, True)

    q = jnp.where(diag, qt_ref[0] * SCALE, 0.0).astype(BF16)

    o_c, imp_rows = _cmp_branch(q, tcol, ck_ref[0], cv_ref[0], mm_ref[...])
    T = DEC_SEQ
    for g in range(KV_HEADS):
        r0 = g * GQA * T
        imp = imp_rows[r0:r0 + T] + imp_rows[r0 + T:r0 + 2 * T] + imp_rows[r0 + 2 * T:r0 + 3 * T] + imp_rows[r0 + 3 * T:r0 + 4 * T]
        bias = _select_bias(imp, tcol[0:T], NSB_S)
        for h in range(GQA):
            a_ref[r0 + h * T:r0 + (h + 1) * T, :] = bias
    a = a_ref[...]

    m_ref[...] = jnp.full(m_ref.shape, NEG, F32)
    l_ref[...] = jnp.zeros(l_ref.shape, F32)
    acc_ref[...] = jnp.zeros(acc_ref.shape, F32)
    allrows = slice(0, R)
    for c in range(SLC_CHUNKS):
        slot = c % 2
        if c + 1 < SLC_CHUNKS:
            chunk_copies(b, c + 1, 1 - slot, True)
        else:
            @pl.when(b + 1 < nb)
            def _():
                chunk_copies(b + 1, 0, 1 - slot, True)
        chunk_copies(b, c, slot, False)
        for t in range(SLC_CHUNK_ROWS // SLC_KT):
            ts = slice(t * SLC_KT, (t + 1) * SLC_KT)
            k_t = kbuf[slot, ts, :].astype(BF16)
            v_t = vbuf[slot, ts, :].astype(BF16)
            onehot = _block_onehot(NSB_S_PAD, SLC_KT, (c * SLC_CHUNK_ROWS + t * SLC_KT) // SEL_BLOCK)
            s = _dot_nt(q, k_t) + _dot(a, onehot)
            _online_update(s, v_t, m_ref, l_ref, acc_ref, allrows)

    newcol = lax.broadcasted_iota(jnp.int32, (1, NEW_PAD), 1)
    new_ok = PAST_LEN + newcol <= tcol
    newk_ref[...] = jnp.zeros(newk_ref.shape, BF16)
    newv_ref[...] = jnp.zeros(newv_ref.shape, BF16)
    newk_ref[0:T, :] = ksn_ref[0].astype(BF16)
    newv_ref[0:T, :] = vsn_ref[0].astype(BF16)
    onehot = _block_onehot(NSB_S_PAD, NEW_PAD, PAST_LEN // SEL_BLOCK)
    s = jnp.where(new_ok, _dot_nt(q, newk_ref[...]) + _dot(a, onehot), NEG)
    _online_update(s, newv_ref[...], m_ref, l_ref, acc_ref, allrows)
    o_s = acc_ref[...] / _lane_tile(l_ref[...], KV_W // LANES)

    wpos = (PAST_LEN - WINDOW) + lax.broadcasted_iota(jnp.int32, (1, WINDOW), 1)
    newk_ref[0:T, :] = kwn_ref[0].astype(BF16)
    newv_ref[0:T, :] = vwn_ref[0].astype(BF16)
    s3 = jnp.concatenate(
        [jnp.where(_window_mask(tcol, wpos), _dot_nt(q, kwc_ref[0, 0].astype(BF16)), NEG),
         jnp.where(new_ok, _dot_nt(q, newk_ref[...]), NEG)], axis=1)
    e3 = jnp.exp(s3 - jnp.max(s3, axis=1, keepdims=True))
    p3 = e3.astype(BF16)
    o_w = (_dot(p3[:, 0:WINDOW], vwc_ref[0, 0].astype(BF16)) + _dot(p3[:, WINDOW:], newv_ref[...]))
    o_w = o_w / jnp.sum(e3, axis=1, keepdims=True)

    sig = _sigmoid(gt_ref[0])
    comb = jnp.where(diag, sig[:, 0:1] * o_c + sig[:, 1:2] * o_s + sig[:, 2:3] * o_w, 0.0)
    folded = comb[:, 0:LANES] + comb[:, LANES:2 * LANES]
    folded = folded + pltpu.roll(folded, HEAD_DIM, 1)
    za = za_ref[0]
    o_ref[0] = folded[:, 0:HEAD_DIM] * (za * _sigmoid(za))


def _sattn(page_table, qt, ck, cv, mmat, cache_k, cache_v, ksn, vsn, win_k, win_v, kwn, vwn, gates, za, layer):
    per_b = lambda shape: pl.BlockSpec((1,) + shape, lambda b, pt: (b,) + (0,) * len(shape))
    win = pl.BlockSpec((1, 1, WINDOW, KV_W), lambda b, pt: (layer, b, 0, 0))
    hbm = pl.BlockSpec(memory_space=pl.ANY)
    return pl.pallas_call(
        functools.partial(_sattn_kernel, layer=layer),
        grid_spec=pltpu.PrefetchScalarGridSpec(
            num_scalar_prefetch=1, grid=(DEC_BATCH,),
            in_specs=[per_b((S_ROWS, KV_W)), per_b((N_CMP, KV_W)), per_b((N_CMP, KV_W)),
                      pl.BlockSpec((N_CMP, NSB_S_PAD), lambda b, pt: (0, 0)),
                      hbm, hbm, per_b((DEC_SEQ, KV_W)), per_b((DEC_SEQ, KV_W)),
                      win, win, per_b((DEC_SEQ, KV_W)), per_b((DEC_SEQ, KV_W)),
                      per_b((S_ROWS, LANES)), per_b((S_ROWS, HEAD_DIM))],
            out_specs=per_b((S_ROWS, HEAD_DIM)),
            scratch_shapes=[pltpu.VMEM((2, SLC_CHUNK_ROWS, KV_W), F32), pltpu.VMEM((2, SLC_CHUNK_ROWS, KV_W), F32),
                            pltpu.SemaphoreType.DMA((2,)), pltpu.SemaphoreType.DMA((2,)),
                            pltpu.VMEM((NEW_PAD, KV_W), BF16), pltpu.VMEM((NEW_PAD, KV_W), BF16),
                            pltpu.VMEM((S_ROWS, NSB_S_PAD), BF16), pltpu.VMEM((S_ROWS, KV_W), F32),
                            pltpu.VMEM((S_ROWS, LANES), F32), pltpu.VMEM((S_ROWS, LANES), F32)]),
        out_shape=jax.ShapeDtypeStruct((DEC_BATCH, S_ROWS, HEAD_DIM), F32),
        compiler_params=pltpu.CompilerParams(dimension_semantics=("arbitrary",), vmem_limit_bytes=VMEM_LIMIT),
        name="attn_sample",
    )(page_table, qt, ck, cv, mmat, cache_k, cache_v, ksn, vsn, win_k, win_v, kwn, vwn, gates, za)


OUT_TM = 256


def _outproj_kernel(h_ref, yc_ref, ya_ref, pe_ref, wc_ref, wa_ref, wg_ref, wp_ref, fg_ref, o_ref, *, final):
    h1 = h_ref[...] + (_dot(yc_ref[...], wc_ref[...]) + _dot(ya_ref[...], wa_ref[...]))
    h2 = h1 + _sigmoid(_dot(h1.astype(BF16), wg_ref[...])) * _dot(pe_ref[...].astype(BF16), wp_ref[...])
    if final:
        h2 = (h2 * lax.rsqrt(jnp.mean(h2 * h2, axis=-1, keepdims=True) + EPS)) * fg_ref[...]
    o_ref[...] = h2


def _outproj(h, yc, ya, pe, wc, wa, wg, wp, fg, final):
    row = lambda w: pl.BlockSpec((OUT_TM, w), lambda i: (i, 0))
    const = lambda shape: pl.BlockSpec(shape, lambda i: (0, 0), pipeline_mode=pl.Buffered(1))
    return pl.pallas_call(
        functools.partial(_outproj_kernel, final=final),
        grid=(N_ROWS // OUT_TM,),
        in_specs=[row(D_MODEL), row(C_CONV), row(ATTN_W), row(PLE_DIM),
                  const((C_CONV, D_MODEL)), const((ATTN_W, D_MODEL)), const((D_MODEL, D_MODEL)),
                  const((PLE_DIM, D_MODEL)), const((1, D_MODEL))],
        out_specs=row(D_MODEL),
        out_shape=jax.ShapeDtypeStruct((N_ROWS, D_MODEL), F32),
        compiler_params=pltpu.CompilerParams(dimension_semantics=("parallel",), vmem_limit_bytes=VMEM_LIMIT),
        name="outproj_final" if final else "outproj",
    )(h, yc, ya, pe, wc, wa, wg, wp, fg)


def _rope_tables():
    pos = jnp.concatenate([jnp.arange(SEQ), PAST_LEN + jnp.tile(jnp.arange(DEC_SEQ), DEC_BATCH)])
    half = ROT_DIM // 2
    inv = ROPE_THETA ** (-2.0 * jnp.arange(half, dtype=F32) / ROT_DIM)
    ang = pos.astype(F32)[:, None] * inv[None, :]
    cos, sin = jnp.cos(ang), jnp.sin(ang)
    one = jnp.ones((N_ROWS, HEAD_DIM - ROT_DIM), F32)
    zero = lambda n: jnp.zeros((N_ROWS, n), F32)
    c = jnp.concatenate([cos, cos, one], axis=1)
    sa = jnp.concatenate([-sin, zero(HEAD_DIM - half)], axis=1)
    sb = jnp.concatenate([zero(half), sin, zero(HEAD_DIM - ROT_DIM)], axis=1)
    return tuple(jnp.concatenate([t, t], axis=1) for t in (c, sa, sb))


def _cmp_to_sel(nsb, pad):
    n = jnp.arange(N_CMP)[:, None] * CMP_STRIDE
    j = jnp.arange(pad)[None, :]
    m = (n < (j + 1) * SEL_BLOCK) & (n + CMP_LEN > j * SEL_BLOCK) & (j < nsb)
    return m.astype(BF16)


def _expand_cmp_weight(w):
    w3 = w.reshape(CMP_LEN, HEAD_DIM, HEAD_DIM)
    eye = jnp.eye(KV_HEADS, dtype=w.dtype)
    halves = [jnp.einsum('lde,gh->lgdhe', w3[s:s + CMP_STRIDE], eye).reshape(CHUNK_W, KV_W)
              for s in (0, CMP_STRIDE)]
    return jnp.concatenate(halves, axis=1).astype(BF16)


def _to_head_rows(x, width):
    return x.reshape(DEC_BATCH, DEC_SEQ, N_HEADS, width).transpose(0, 2, 1, 3).reshape(DEC_BATCH, S_ROWS, width)


def kernel(x_prompt, x_sample, cache_cmp_k, cache_cmp_v, cache_slc_k, cache_slc_v, cache_win_k, cache_win_v,
           state_conv, page_table, p_prompt, p_sample, norm_g, w_in, conv_w, conv_b, conv_ln_g, conv_ln_b,
           w_cmp_k, w_cmp_v, w_out, w_ple, w_ple_gate, final_g):
    n_pool = cache_cmp_k.shape[1]
    h = jnp.concatenate([x_prompt.reshape(SEQ, D_MODEL), x_sample.reshape(N_SAMPLE, D_MODEL)], axis=0)
    cos, sa, sb = _rope_tables()
    mm_p = _cmp_to_sel(NSB_P, NSB_P)
    mm_s = _cmp_to_sel(NSB_S, NSB_S_PAD)
    chunked = lambda c: c.reshape(DEPTH, n_pool, PAGE_CHUNKS, CHUNK_W)
    paged = lambda c: c.reshape(DEPTH, n_pool, PAGE_SIZE, KV_W)
    cck, ccv = chunked(cache_cmp_k), chunked(cache_cmp_v)
    csk, csv = paged(cache_slc_k), paged(cache_slc_v)
    cwk = cache_win_k.reshape(DEPTH, DEC_BATCH, WINDOW, KV_W)
    cwv = cache_win_v.reshape(DEPTH, DEC_BATCH, WINDOW, KV_W)
    hist_p = jnp.zeros((1, HIST_PAD, C_CONV), F32)
    hist_s = jnp.pad(state_conv, ((0, 0), (0, 0), (HIST_OFF, 0), (0, 0)))

    o_q = 3 * C_CONV
    o_kv = o_q + ATTN_W
    o_gt = o_kv + 6 * KV_W
    o_za = o_gt + 3 * N_HEADS
    kv = lambda wi, n: wi[:, o_kv + n * KV_W:o_kv + (n + 1) * KV_W]

    st_p, st_s = [], []
    for i in range(DEPTH):
        wi = w_in[i]
        w_a = jnp.concatenate([wi[:, :o_q], wi[:, o_za:]], axis=1).astype(BF16)
        w_b = jnp.concatenate([wi[:, o_q:o_kv], kv(wi, 0), kv(wi, 2), kv(wi, 4), kv(wi, 1), kv(wi, 3), kv(wi, 5)],
                              axis=1).astype(BF16)
        w_g = jnp.pad(wi[:, o_gt:o_za], ((0, 0), (0, LANES - 3 * N_HEADS))).astype(BF16)
        g_row = norm_g[i].reshape(1, D_MODEL)

        ua, = _inproj(h, g_row, w_a)
        ub, ubb, gates = _inproj(h, g_row, w_b, extras=(w_g, cos, sa, sb))

        conv_args = (conv_w[i], conv_b[i].reshape(1, C_CONV), conv_ln_g[i].reshape(1, C_CONV),
                     conv_ln_b[i].reshape(1, C_CONV))
        yc_p, nc_p = _conv(ua, hist_p, *conv_args, row0=0, batch=1, t_len=SEQ, tm=256)
        yc_s, nc_s = _conv(ua, hist_s[i], *conv_args, row0=SEQ, batch=DEC_BATCH, t_len=DEC_SEQ, tm=DEC_SEQ)

        seg = lambda n: ub[:, ATTN_W + n * KV_W:ATTN_W + (n + 1) * KV_W]
        kc, ks, kw, vc, vs, vw = (seg(n) for n in range(6))
        wab_k, wab_v = _expand_cmp_weight(w_cmp_k[i]), _expand_cmp_weight(w_cmp_v[i])

        ck_p = _pcompress(kc[:SEQ].reshape(N_CHUNK, CHUNK_W), wab_k)
        cv_p = _pcompress(vc[:SEQ].reshape(N_CHUNK, CHUNK_W), wab_v)
        ya_p = _pattn(ub, ck_p, cv_p, mm_p, ubb, gates, ua)

        new_chunk = lambda x: jnp.pad(x[SEQ:].reshape(DEC_BATCH, 1, DEC_SEQ * KV_W),
                                      ((0, 0), (0, SUBLANES - 1), (0, CHUNK_W - DEC_SEQ * KV_W)))
        ck_s = _scompress(page_table, cck, new_chunk(kc), wab_k, i)
        cv_s = _scompress(page_table, ccv, new_chunk(vc), wab_v, i)
        new_rows = lambda x: x[SEQ:].reshape(DEC_BATCH, DEC_SEQ, KV_W)
        qt = jnp.tile(_to_head_rows(ub[SEQ:, :ATTN_W], HEAD_DIM), (1, 1, KV_HEADS))
        gt_s = gates[SEQ:, :3 * N_HEADS].reshape(DEC_BATCH, DEC_SEQ, 3, N_HEADS).transpose(0, 3, 1, 2)
        gt_s = jnp.pad(gt_s.reshape(DEC_BATCH, S_ROWS, 3), ((0, 0), (0, 0), (0, LANES - 3)))
        za_s = _to_head_rows(ua[SEQ:, 3 * C_CONV:], HEAD_DIM)
        ya_s = _sattn(page_table, qt, ck_s, cv_s, mm_s, csk, csv, new_rows(ks), new_rows(vs),
                      cwk, cwv, new_rows(kw), new_rows(vw), gt_s, za_s, i)
        ya_s = ya_s.reshape(DEC_BATCH, N_HEADS, DEC_SEQ, HEAD_DIM).transpose(0, 2, 1, 3).reshape(N_SAMPLE, ATTN_W)

        yc = jnp.concatenate([yc_p, yc_s.astype(BF16)], axis=0)
        ya = jnp.concatenate([ya_p, ya_s.astype(BF16)], axis=0)
        pe = jnp.concatenate([p_prompt[i].reshape(SEQ, PLE_DIM), p_sample[i].reshape(N_SAMPLE, PLE_DIM)], axis=0)
        wo = w_out[i].astype(BF16)
        h = _outproj(h, yc, ya, pe, wo[:C_CONV], wo[C_CONV:], w_ple_gate[i].astype(BF16), w_ple[i].astype(BF16),
                     final_g.reshape(1, D_MODEL), final=(i == DEPTH - 1))

        heads = lambda x, b, t: x.reshape(b, t, KV_HEADS, HEAD_DIM)
        wb = min(WINDOW, SEQ)
        st_p.append((heads(kc[:SEQ], 1, SEQ), heads(vc[:SEQ], 1, SEQ), heads(ks[:SEQ], 1, SEQ), heads(vs[:SEQ], 1, SEQ),
                     heads(kw[SEQ - wb:SEQ], 1, wb), heads(vw[SEQ - wb:SEQ], 1, wb), nc_p[:, HIST_OFF:]))
        hs = lambda x: heads(x[SEQ:], DEC_BATCH, DEC_SEQ)
        st_s.append((hs(kc), hs(vc), hs(ks), hs(vs),
                     jnp.concatenate([cache_win_k[i][:, DEC_SEQ:], hs(kw)], axis=1),
                     jnp.concatenate([cache_win_v[i][:, DEC_SEQ:], hs(vw)], axis=1), nc_s[:, HIST_OFF:]))

    y_prompt = h[:SEQ].reshape(1, SEQ, D_MODEL)
    y_sample = h[SEQ:].reshape(DEC_BATCH, DEC_SEQ, D_MODEL)
    outs_p = [jnp.stack([s[n] for s in st_p]) for n in range(7)]
    outs_s = [jnp.stack([s[n] for s in st_s]) for n in range(7)]
    return (y_prompt, y_sample, *outs_p, *outs_s)
```

```python
import functools

import jax
import jax.numpy as jnp
from jax import lax
from jax.experimental import pallas as pl
from jax.experimental.pallas import tpu as pltpu

F32 = jnp.float32
BF16 = jnp.bfloat16

D_MODEL = 2048
SEQ = 8192
DEPTH = 2
DEC_BATCH = 32
DEC_SEQ = 8
PAST_LEN = 8192
PAGE_SIZE = 128
HEAD_DIM = 64
ATTN_W = D_MODEL // 2
C_CONV = D_MODEL - ATTN_W
N_HEADS = ATTN_W // HEAD_DIM
KV_HEADS = N_HEADS // 4
GQA = N_HEADS // KV_HEADS
KV_W = KV_HEADS * HEAD_DIM
ROT_DIM = HEAD_DIM // 4
ROPE_THETA = 500000.0
CONV_WIDTH = 31
CMP_LEN = 32
CMP_STRIDE = 16
SEL_BLOCK = 64
SEL_TOPK = 16
WINDOW = 512
Q_BLOCK = 128
PLE_DIM = 256
EPS = 1e-6
NEG = -1e30
BIG = 1e9

N_SAMPLE = DEC_BATCH * DEC_SEQ
N_ROWS = SEQ + N_SAMPLE
N_PAGES = PAST_LEN // PAGE_SIZE
N_CHUNK = PAST_LEN // CMP_STRIDE
N_CMP = N_CHUNK
CHUNK_W = CMP_STRIDE * KV_W
PAGE_CHUNKS = PAGE_SIZE // CMP_STRIDE
NSB_P = SEQ // SEL_BLOCK
NSB_S = -(-(PAST_LEN + DEC_SEQ) // SEL_BLOCK)
NSB_S_PAD = 256
BELOW_NEG = -3e38
SCALE = HEAD_DIM ** -0.5

LANES = 128
SUBLANES = 8
VMEM_LIMIT = 56 * 1024 * 1024

A_COLS = 4 * C_CONV
B_COLS = ATTN_W + 6 * KV_W
B_ROPE_COLS = ATTN_W + 3 * KV_W
B_KC, B_KS, B_KW, B_VC, B_VS, B_VW = (ATTN_W // KV_W + n for n in range(6))
ROW_TILE = N_ROWS // 8
A_TN = 512
B_TN = 256
SLC_KT = 512


def _dot(a, b):
    return jnp.dot(a, b, preferred_element_type=F32)


def _dot_nt(a, b):
    return lax.dot_general(a, b, (((1,), (1,)), ((), ())), preferred_element_type=F32)


def _lane_tile(x, n):
    return x if n == 1 else jnp.concatenate([x] * n, axis=1)


def _sigmoid(x):
    return 1.0 / (1.0 + jnp.exp(-x))


def _inproj_kernel(*refs, rope_tiles, extras):
    if extras:
        x_ref, g_ref, w_ref, wg_ref, cos_ref, sa_ref, sb_ref, u_ref, ub_ref, gate_ref, hn_ref = refs
    else:
        x_ref, g_ref, w_ref, u_ref, hn_ref = refs
    j = pl.program_id(1)

    @pl.when(j == 0)
    def _():
        x = x_ref[...]
        r = lax.rsqrt(jnp.mean(x * x, axis=-1, keepdims=True) + EPS)
        hn = ((x * r) * g_ref[...]).astype(BF16)
        hn_ref[...] = hn
        if extras:
            gate_ref[...] = _dot(hn, wg_ref[...])

    acc = _dot(hn_ref[...], w_ref[...])
    if not extras:
        u_ref[...] = acc
        return

    @pl.when(j < rope_tiles)
    def _():
        c, sa, sb = cos_ref[...], sa_ref[...], sb_ref[...]
        for k in range(acc.shape[1] // LANES):
            cs = slice(k * LANES, (k + 1) * LANES)
            xs = acc[:, cs]
            y = xs * c + pltpu.roll(xs, LANES - ROT_DIM // 2, 1) * sa + pltpu.roll(xs, ROT_DIM // 2, 1) * sb
            u_ref[:, cs] = y
            ub_ref[:, cs] = y.astype(BF16)

    @pl.when(j >= rope_tiles)
    def _():
        u_ref[...] = acc
        ub_ref[...] = acc.astype(BF16)


def _inproj(x, g, w, extras=None):
    n_cols = w.shape[1]
    tn = B_TN if extras else A_TN
    grid = (N_ROWS // ROW_TILE, n_cols // tn)
    row = lambda i, j: (i, 0)
    in_specs = [pl.BlockSpec((ROW_TILE, D_MODEL), row),
                pl.BlockSpec((1, D_MODEL), lambda i, j: (0, 0)),
                pl.BlockSpec((D_MODEL, tn), lambda i, j: (0, j))]
    out_specs = [pl.BlockSpec((ROW_TILE, tn), lambda i, j: (i, j))]
    out_shape = [jax.ShapeDtypeStruct((N_ROWS, n_cols), F32)]
    args = [x, g, w]
    if extras:
        wg, cos, sa, sb = extras
        in_specs += [pl.BlockSpec((D_MODEL, LANES), lambda i, j: (0, 0))] + [pl.BlockSpec((ROW_TILE, LANES), row)] * 3
        out_specs += [pl.BlockSpec((ROW_TILE, tn), lambda i, j: (i, j)), pl.BlockSpec((ROW_TILE, LANES), row)]
        out_shape += [jax.ShapeDtypeStruct((N_ROWS, n_cols), BF16), jax.ShapeDtypeStruct((N_ROWS, LANES), F32)]
        args += [wg, cos, sa, sb]
    return pl.pallas_call(
        functools.partial(_inproj_kernel, rope_tiles=B_ROPE_COLS // B_TN, extras=bool(extras)),
        grid=grid, in_specs=in_specs, out_specs=out_specs, out_shape=out_shape,
        scratch_shapes=[pltpu.VMEM((ROW_TILE, D_MODEL), BF16)],
        compiler_params=pltpu.CompilerParams(dimension_semantics=("parallel", "arbitrary"),
                                             vmem_limit_bytes=VMEM_LIMIT),
        name="inproj_b" if extras else "inproj_a",
    )(*args)


HIST_PAD = 32
HIST_OFF = HIST_PAD - (CONV_WIDTH - 1)


def _conv_kernel(a_ref, b_ref, zc_ref, hist_ref, w_ref, cb_ref, lg_ref, lb_ref, yc_ref, nc_ref, ext_ref, y_ref, *, tm):
    t = pl.program_id(1)

    @pl.when(t == 0)
    def _():
        ext_ref[0:HIST_PAD, :] = hist_ref[0]

    ext_ref[HIST_PAD:HIST_PAD + tm, :] = a_ref[...] * _sigmoid(b_ref[...])
    for c in range(C_CONV // LANES):
        cs = slice(c * LANES, (c + 1) * LANES)
        acc = jnp.zeros((tm, LANES), F32) + cb_ref[:, cs]
        for k in range(CONV_WIDTH):
            acc = acc + ext_ref[HIST_OFF + k:HIST_OFF + k + tm, cs] * w_ref[k:k + 1, cs]
        y_ref[:, cs] = acc
    y = y_ref[...]
    mu = jnp.mean(y, axis=-1, keepdims=True)
    yc = y - mu
    yn = yc * lax.rsqrt(jnp.mean(yc * yc, axis=-1, keepdims=True) + EPS) * lg_ref[...] + lb_ref[...]
    zc = zc_ref[...]
    yc_ref[...] = ((yn * _sigmoid(yn)) * (zc * _sigmoid(zc))).astype(yc_ref.dtype)
    tail = ext_ref[tm:tm + HIST_PAD, :]
    ext_ref[0:HIST_PAD, :] = tail

    @pl.when(t == pl.num_programs(1) - 1)
    def _():
        nc_ref[0] = tail


def _conv(ua, hist, w, cb, lg, lb, *, row0, batch, t_len, tm):
    nt = t_len // tm
    blk0 = row0 // tm
    col = lambda c: (lambda b, t: (blk0 + b * nt + t, c))
    vec = pl.BlockSpec((1, C_CONV), lambda b, t: (0, 0))
    return pl.pallas_call(
        functools.partial(_conv_kernel, tm=tm),
        grid=(batch, nt),
        in_specs=[pl.BlockSpec((tm, C_CONV), col(0)), pl.BlockSpec((tm, C_CONV), col(1)),
                  pl.BlockSpec((tm, C_CONV), col(2)),
                  pl.BlockSpec((1, HIST_PAD, C_CONV), lambda b, t: (b, 0, 0)),
                  pl.BlockSpec((CONV_WIDTH, C_CONV), lambda b, t: (0, 0)), vec, vec, vec],
        out_specs=[pl.BlockSpec((tm, C_CONV), lambda b, t: (b * nt + t, 0)),
                   pl.BlockSpec((1, HIST_PAD, C_CONV), lambda b, t: (b, 0, 0))],
        out_shape=[jax.ShapeDtypeStruct((batch * t_len, C_CONV), BF16 if tm % 16 == 0 else F32),
                   jax.ShapeDtypeStruct((batch, HIST_PAD, C_CONV), F32)],
        scratch_shapes=[pltpu.VMEM((HIST_PAD + tm, C_CONV), F32), pltpu.VMEM((tm, C_CONV), F32)],
        compiler_params=pltpu.CompilerParams(dimension_semantics=("parallel", "arbitrary"),
                                             vmem_limit_bytes=VMEM_LIMIT),
        name="conv_group",
    )(ua, ua, ua, hist, w, cb, lg, lb)


CMP_ROWS = PAST_LEN + CMP_STRIDE


LANE_HALVES = KV_W // LANES


def _compress_rows(x_ref, w_ref):
    acc = None
    for l in range(CMP_LEN):
        xl = jnp.concatenate([x_ref[c, pl.ds(l, N_CMP, stride=CMP_STRIDE), :] for c in range(LANE_HALVES)], axis=1)
        part = _dot(xl.astype(BF16), w_ref[l])
        acc = part if acc is None else acc + part
    return acc


def _pcompress_kernel(x_ref, w_ref, o_ref, rows_ref):
    for c in range(LANE_HALVES):
        rows_ref[c, 0:SEQ, :] = x_ref[:, c * LANES:(c + 1) * LANES]
        rows_ref[c, SEQ:, :] = jnp.zeros((CMP_ROWS - SEQ, LANES), F32)
    o_ref[...] = _compress_rows(rows_ref, w_ref).astype(o_ref.dtype)


def _pcompress(ub, col_block, w):
    once = pl.Buffered(1)
    return pl.pallas_call(
        _pcompress_kernel,
        grid=(1,),
        in_specs=[pl.BlockSpec((SEQ, KV_W), lambda i: (0, col_block), pipeline_mode=once),
                  pl.BlockSpec((CMP_LEN, KV_W, KV_W), lambda i: (0, 0, 0), pipeline_mode=once)],
        out_specs=pl.BlockSpec((N_CMP, KV_W), lambda i: (0, 0)),
        out_shape=jax.ShapeDtypeStruct((N_CMP, KV_W), BF16),
        scratch_shapes=[pltpu.VMEM((LANE_HALVES, CMP_ROWS, LANES), F32)],
        compiler_params=pltpu.CompilerParams(dimension_semantics=("arbitrary",), vmem_limit_bytes=VMEM_LIMIT),
        name="compress_prompt",
    )(ub, w)


def _scompress_kernel(pt_ref, cache_ref, new_ref, w_ref, o_ref, tbuf, rows_ref, sem, *, layer):
    b = pl.program_id(0)
    slot = b % 2

    def copies(bb, sl, go):
        def body(p, carry):
            cp = pltpu.make_async_copy(cache_ref.at[layer, pt_ref[bb, p]],
                                       tbuf.at[sl, :, pl.ds(pl.multiple_of(p * PAGE_SIZE, PAGE_SIZE), PAGE_SIZE)],
                                       sem.at[sl])
            cp.start() if go else cp.wait()
            return carry
        lax.fori_loop(0, N_PAGES, body, 0)

    @pl.when(b == 0)
    def _():
        copies(0, 0, True)

    @pl.when(b + 1 < pl.num_programs(0))
    def _():
        copies(b + 1, 1 - slot, True)

    copies(b, slot, False)
    for c in range(LANE_HALVES):
        cs = slice(c * LANES, (c + 1) * LANES)
        for p in range(N_PAGES):
            ps = slice(p * PAGE_SIZE, (p + 1) * PAGE_SIZE)
            rows_ref[c, ps, :] = tbuf[slot, cs, ps].T
        rows_ref[c, PAST_LEN:PAST_LEN + DEC_SEQ, :] = new_ref[0, :, cs]
        rows_ref[c, PAST_LEN + DEC_SEQ:, :] = jnp.zeros((CMP_ROWS - PAST_LEN - DEC_SEQ, LANES), F32)
    o_ref[0] = _compress_rows(rows_ref, w_ref).astype(o_ref.dtype)


def _scompress(page_table, cache_t, new_rows, w, layer):
    return pl.pallas_call(
        functools.partial(_scompress_kernel, layer=layer),
        grid_spec=pltpu.PrefetchScalarGridSpec(
            num_scalar_prefetch=1, grid=(DEC_BATCH,),
            in_specs=[pl.BlockSpec(memory_space=pl.ANY),
                      pl.BlockSpec((1, DEC_SEQ, KV_W), lambda b, pt: (b, 0, 0)),
                      pl.BlockSpec((CMP_LEN, KV_W, KV_W), lambda b, pt: (0, 0, 0), pipeline_mode=pl.Buffered(1))],
            out_specs=pl.BlockSpec((1, N_CMP, KV_W), lambda b, pt: (b, 0, 0)),
            scratch_shapes=[pltpu.VMEM((2, KV_W, PAST_LEN), F32), pltpu.VMEM((LANE_HALVES, CMP_ROWS, LANES), F32),
                            pltpu.SemaphoreType.DMA((2,))]),
        out_shape=jax.ShapeDtypeStruct((DEC_BATCH, N_CMP, KV_W), BF16),
        compiler_params=pltpu.CompilerParams(dimension_semantics=("arbitrary",), vmem_limit_bytes=VMEM_LIMIT),
        name="compress_sample",
    )(page_table, cache_t, new_rows, w)


def _cmp_branch(q, tcol, ck, cv):
    s = _dot_nt(q, ck)
    n = lax.broadcasted_iota(jnp.int32, s.shape, 1)
    vis = (n * CMP_STRIDE + (CMP_LEN - 1)) <= tcol
    sm = jnp.where(vis, s, NEG)
    e = jnp.where(vis, jnp.exp(sm - jnp.max(sm, axis=1, keepdims=True)), 0.0)
    den = jnp.sum(e, axis=1, keepdims=True)
    p = e / jnp.where(den > 0.0, den, 1.0)
    return _dot(p.astype(BF16), cv), p


def _select_bias(imp, t, nsb, axis):
    j = lax.broadcasted_iota(jnp.int32, imp.shape, axis)
    cur = lax.shift_right_arithmetic(t, 6)
    forced = (j == 0) | (j == cur) | (j == cur - 1)
    score = jnp.where(j * SEL_BLOCK <= t, jnp.where(forced, BIG, imp), NEG)
    score = jnp.where(j < nsb, score, BELOW_NEG)
    jf = j.astype(F32)
    bias = jnp.full(imp.shape, NEG, F32)
    for _ in range(SEL_TOPK):
        mx = jnp.max(score, axis=axis, keepdims=True)
        idx = jnp.min(jnp.where(score == mx, jf, 1e9), axis=axis, keepdims=True)
        pick = jf == idx
        bias = jnp.where(pick, 0.0, bias)
        score = jnp.where(pick, BELOW_NEG, score)
    return bias


def _block_onehot(nb, kt, first_block):
    j = lax.broadcasted_iota(jnp.int32, (nb, kt), 0)
    col = lax.broadcasted_iota(jnp.int32, (nb, kt), 1)
    return jnp.where(j - lax.shift_right_arithmetic(col, 6) == first_block, 1.0, 0.0).astype(BF16)


def _online_update(s, v, m_ref, l_ref, acc_ref, rows, v_transposed=False):
    m_prev, l_prev = m_ref[rows, :], l_ref[rows, :]
    m_next = jnp.maximum(m_prev, jnp.max(s, axis=1, keepdims=True))
    alpha = jnp.exp(m_prev - m_next)
    p = jnp.exp(s - _lane_tile(m_next, s.shape[1] // LANES))
    l_ref[rows, :] = alpha * l_prev + jnp.sum(p, axis=1, keepdims=True)
    m_ref[rows, :] = m_next
    pv = _dot_nt(p.astype(BF16), v) if v_transposed else _dot(p.astype(BF16), v)
    acc_ref[rows, :] = acc_ref[rows, :] * _lane_tile(alpha, KV_W // LANES) + pv


def _window_mask(tcol, kpos):
    return lax.bitcast_convert_type(tcol - kpos, jnp.uint32) <= jnp.uint32(WINDOW)


GROUP_ROWS = GQA * Q_BLOCK


PACK_TM = 512


def _pack_keys_kernel(k_ref, o_ref):
    i = pl.program_id(0)
    lane = lax.broadcasted_iota(jnp.int32, (PACK_TM, LANES), 1)
    blk = lax.shift_right_arithmetic(i * PACK_TM + lax.broadcasted_iota(jnp.int32, (PACK_TM, LANES), 0), 6)
    lo = lane < HEAD_DIM
    hot_lo = jnp.where(lane - HEAD_DIM == blk, 1.0, 0.0)
    hot_hi = jnp.where(lane + HEAD_DIM == blk, 1.0, 0.0)
    for g in range(KV_HEADS):
        pair = k_ref[:, (g // 2) * LANES:(g // 2 + 1) * LANES]
        if g % 2:
            pair = pltpu.roll(pair, HEAD_DIM, 1)
        o_ref[g, :, 0:LANES] = jnp.where(lo, pair, hot_lo).astype(BF16)
        o_ref[g, :, LANES:2 * LANES] = jnp.where(lo, hot_hi, 0.0).astype(BF16)


def _pack_keys(ub, col_block):
    assert NSB_P == 2 * HEAD_DIM and KV_W == 2 * LANES
    return pl.pallas_call(
        _pack_keys_kernel,
        grid=(SEQ // PACK_TM,),
        in_specs=[pl.BlockSpec((PACK_TM, KV_W), lambda i: (i, col_block))],
        out_specs=pl.BlockSpec((KV_HEADS, PACK_TM, KV_W), lambda i: (0, i, 0)),
        out_shape=jax.ShapeDtypeStruct((KV_HEADS, SEQ, KV_W), BF16),
        compiler_params=pltpu.CompilerParams(dimension_semantics=("parallel",), vmem_limit_bytes=VMEM_LIMIT),
        name="pack_keys",
    )(ub)


def _pattn_kernel(q_ref, ck_ref, cv_ref, mm_ref, ka_ref, vs_ref, kw_ref, vw_ref, gt_ref, za_ref, o_ref,
                  qbd_ref, qa_ref, oc_ref, acc_ref, m_ref, l_ref):
    i = pl.program_id(0)
    q0 = i * Q_BLOCK
    T, R = Q_BLOCK, GROUP_ROWS
    lane = lax.broadcasted_iota(jnp.int32, (T, LANES), 1)
    half = lax.shift_right_arithmetic(lane, 6)
    lo = lane < HEAD_DIM
    tcol = q0 + (lax.broadcasted_iota(jnp.int32, (R, 1), 0) & (T - 1))
    trow = q0 + lax.broadcasted_iota(jnp.int32, (1, T), 1)

    zeros = jnp.zeros((T, LANES), BF16)
    for hh in range(N_HEADS):
        g = hh // GQA
        pair = q_ref[:, (hh // 2) * LANES:(hh // 2 + 1) * LANES] * SCALE
        if hh % 2 != g % 2:
            pair = pltpu.roll(pair, HEAD_DIM, 1)
        x = jnp.where(half == g % 2, pair, 0.0).astype(BF16)
        qbd_ref[hh * T:(hh + 1) * T, 0:LANES] = x if g // 2 == 0 else zeros
        qbd_ref[hh * T:(hh + 1) * T, LANES:2 * LANES] = zeros if g // 2 == 0 else x

    for g in range(KV_HEADS):
        rows = slice(g * R, (g + 1) * R)
        o_c, p = _cmp_branch(qbd_ref[rows, :], tcol, ck_ref[...], cv_ref[...])
        oc_ref[rows, :] = o_c
        imp_rows = _dot(p.astype(BF16), mm_ref[...])
        imp = imp_rows[0:T] + imp_rows[T:2 * T] + imp_rows[2 * T:3 * T] + imp_rows[3 * T:4 * T]
        bias = _select_bias(imp.T, trow, NSB_P, axis=0).T
        bias = pltpu.roll(bias, HEAD_DIM, 1)
        for h in range(GQA):
            hh = g * GQA + h
            pair = q_ref[:, (hh // 2) * LANES:(hh // 2 + 1) * LANES] * SCALE
            if hh % 2:
                pair = pltpu.roll(pair, HEAD_DIM, 1)
            qa_ref[hh * T:(hh + 1) * T, 0:LANES] = jnp.where(lo, pair, bias).astype(BF16)
            qa_ref[hh * T:(hh + 1) * T, LANES:2 * LANES] = jnp.where(lo, bias, 0.0).astype(BF16)

    m_ref[...] = jnp.full(m_ref.shape, NEG, F32)
    l_ref[...] = jnp.zeros(l_ref.shape, F32)
    acc_ref[...] = jnp.zeros(acc_ref.shape, F32)
    kcol = lax.broadcasted_iota(jnp.int32, (1, SLC_KT), 1)

    def slc_tile(c, causal):
        k0 = pl.multiple_of(c * SLC_KT, SLC_KT)
        v_t = vs_ref[pl.ds(k0, SLC_KT), :]
        for g in range(KV_HEADS):
            rows = slice(g * R, (g + 1) * R)
            s = _dot_nt(qa_ref[rows, :], ka_ref[g, pl.ds(k0, SLC_KT), :])
            if causal:
                s = jnp.where(k0 + kcol <= tcol, s, NEG)
            _online_update(s, v_t, m_ref, l_ref, acc_ref, rows)

    c_last = (q0 + T - 1) // SLC_KT

    def full_tile(c, carry):
        slc_tile(c, False)
        return carry

    lax.fori_loop(0, c_last, full_tile, 0)
    slc_tile(c_last, True)

    start = pl.multiple_of(jnp.maximum(q0 - WINDOW, 0), Q_BLOCK)
    kw_t = kw_ref[pl.ds(start, WINDOW + T), :]
    vw_t = vw_ref[pl.ds(start, WINDOW + T), :]
    kpos = start + lax.broadcasted_iota(jnp.int32, (1, WINDOW + T), 1)
    sig = _sigmoid(gt_ref[...])
    for g in range(KV_HEADS):
        rows = slice(g * R, (g + 1) * R)
        cc = slice((g // 2) * LANES, (g // 2 + 1) * LANES)
        s3 = jnp.where(_window_mask(tcol, kpos), _dot_nt(qbd_ref[rows, :], kw_t), NEG)
        e3 = jnp.exp(s3 - jnp.max(s3, axis=1, keepdims=True))
        o_w = _dot(e3.astype(BF16), vw_t)[:, cc] / jnp.sum(e3, axis=1, keepdims=True)
        o_s = acc_ref[rows, cc] / l_ref[rows, :]
        o_c = oc_ref[rows, cc]
        for pair in range(2):
            pieces = []
            for sub in range(2):
                h = 2 * pair + sub
                hh = g * GQA + h
                hr = slice(h * T, (h + 1) * T)
                comb = (sig[:, hh:hh + 1] * o_c[hr] + sig[:, N_HEADS + hh:N_HEADS + hh + 1] * o_s[hr]
                        + sig[:, 2 * N_HEADS + hh:2 * N_HEADS + hh + 1] * o_w[hr])
                pieces.append(comb if g % 2 == sub else pltpu.roll(comb, HEAD_DIM, 1))
            oc_cols = slice((2 * g + pair) * LANES, (2 * g + pair + 1) * LANES)
            za = za_ref[:, oc_cols]
            o_ref[:, oc_cols] = (jnp.where(half == 0, pieces[0], pieces[1]) * (za * _sigmoid(za))).astype(BF16)


def _pattn(ub, ck, cv, mmat, kaug, ubb, gates, ua):
    const = lambda r, c: (lambda i: (r, c))
    once = pl.Buffered(1)
    resident = lambda c: pl.BlockSpec((SEQ, KV_W), const(0, c), pipeline_mode=once)
    small = lambda shape: pl.BlockSpec(shape, const(0, 0))
    return pl.pallas_call(
        _pattn_kernel,
        grid=(SEQ // Q_BLOCK,),
        in_specs=[pl.BlockSpec((Q_BLOCK, ATTN_W), lambda i: (i, 0)),
                  small((N_CMP, KV_W)), small((N_CMP, KV_W)), small((N_CMP, NSB_P)),
                  pl.BlockSpec((KV_HEADS, SEQ, KV_W), lambda i: (0, 0, 0), pipeline_mode=once),
                  resident(B_VS), resident(B_KW), resident(B_VW),
                  pl.BlockSpec((Q_BLOCK, LANES), lambda i: (i, 0)),
                  pl.BlockSpec((Q_BLOCK, ATTN_W), lambda i: (i, 3))],
        out_specs=pl.BlockSpec((Q_BLOCK, ATTN_W), lambda i: (i, 0)),
        out_shape=jax.ShapeDtypeStruct((SEQ, ATTN_W), BF16),
        scratch_shapes=[pltpu.VMEM((N_HEADS * Q_BLOCK, KV_W), BF16),
                        pltpu.VMEM((N_HEADS * Q_BLOCK, KV_W), BF16),
                        pltpu.VMEM((N_HEADS * Q_BLOCK, KV_W), F32),
                        pltpu.VMEM((N_HEADS * Q_BLOCK, KV_W), F32),
                        pltpu.VMEM((N_HEADS * Q_BLOCK, LANES), F32),
                        pltpu.VMEM((N_HEADS * Q_BLOCK, LANES), F32)],
        compiler_params=pltpu.CompilerParams(dimension_semantics=("parallel",), vmem_limit_bytes=VMEM_LIMIT),
        name="attn_prompt",
    )(ub, ck, cv, mmat, kaug, ubb, ubb, ubb, gates, ua)


SLC_PAGES = 16
SLC_CHUNKS = N_PAGES // SLC_PAGES
SLC_CHUNK_ROWS = SLC_PAGES * PAGE_SIZE
S_ROWS = N_HEADS * DEC_SEQ
NEW_PAD = LANES


def _sattn_kernel(pt_ref, qt_ref, ck_ref, cv_ref, mm_ref, ksc_ref, vsc_ref, ksn_ref, vsn_ref,
                  kwc_ref, vwc_ref, kwn_ref, vwn_ref, gt_ref, za_ref, o_ref,
                  kbuf, vbuf, ksem, vsem, newk_ref, newv_ref, a_ref, acc_ref, m_ref, l_ref, *, layer):
    b = pl.program_id(0)
    nb = pl.num_programs(0)
    R = S_ROWS
    row = lax.broadcasted_iota(jnp.int32, (R, 1), 0)
    tcol = PAST_LEN + (row & (DEC_SEQ - 1))
    lane_group = lax.shift_right_arithmetic(lax.broadcasted_iota(jnp.int32, (R, KV_W), 1), 6)
    diag = lane_group == lax.shift_right_arithmetic(row, 5)

    def chunk_copies(bb, c, slot, go):
        def body(p, carry):
            for cache, buf, sem in ((ksc_ref, kbuf, ksem), (vsc_ref, vbuf, vsem)):
                cp = pltpu.make_async_copy(cache.at[layer, pt_ref[bb, c * SLC_PAGES + p]],
                                           buf.at[slot, :, pl.ds(pl.multiple_of(p * PAGE_SIZE, PAGE_SIZE), PAGE_SIZE)],
                                           sem.at[slot])
                cp.start() if go else cp.wait()
            return carry
        lax.fori_loop(0, SLC_PAGES, body, 0)

    @pl.when(b == 0)
    def _():
        chunk_copies(0, 0, 0, True)

    q = jnp.where(diag, qt_ref[0] * SCALE, 0.0).astype(BF16)

    o_c, p = _cmp_branch(q, tcol, ck_ref[0], cv_ref[0])
    T = DEC_SEQ
    p_group = jnp.concatenate(
        [p[r0:r0 + T] + p[r0 + T:r0 + 2 * T] + p[r0 + 2 * T:r0 + 3 * T] + p[r0 + 3 * T:r0 + 4 * T]
         for r0 in range(0, R, GQA * T)], axis=0)
    imp = _dot(p_group.astype(BF16), mm_ref[...])
    bias = _select_bias(imp, tcol[0:KV_HEADS * T], NSB_S, axis=1)
    for hh in range(N_HEADS):
        g = hh // GQA
        a_ref[hh * T:(hh + 1) * T, :] = bias[g * T:(g + 1) * T].astype(BF16)
    a = a_ref[...]

    m_ref[...] = jnp.full(m_ref.shape, NEG, F32)
    l_ref[...] = jnp.zeros(l_ref.shape, F32)
    acc_ref[...] = jnp.zeros(acc_ref.shape, F32)
    allrows = slice(0, R)
    for c in range(SLC_CHUNKS):
        slot = c % 2
        if c + 1 < SLC_CHUNKS:
            chunk_copies(b, c + 1, 1 - slot, True)
        else:
            @pl.when(b + 1 < nb)
            def _():
                chunk_copies(b + 1, 0, 1 - slot, True)
        chunk_copies(b, c, slot, False)
        for t in range(SLC_CHUNK_ROWS // SLC_KT):
            ts = slice(t * SLC_KT, (t + 1) * SLC_KT)
            onehot = _block_onehot(NSB_S_PAD, SLC_KT, (c * SLC_CHUNK_ROWS + t * SLC_KT) // SEL_BLOCK)
            s = _dot(q, kbuf[slot, :, ts].astype(BF16)) + _dot(a, onehot)
            _online_update(s, vbuf[slot, :, ts].astype(BF16), m_ref, l_ref, acc_ref, allrows, v_transposed=True)

    newcol = lax.broadcasted_iota(jnp.int32, (1, NEW_PAD), 1)
    new_ok = PAST_LEN + newcol <= tcol
    newk_ref[...] = jnp.zeros(newk_ref.shape, BF16)
    newv_ref[...] = jnp.zeros(newv_ref.shape, BF16)
    newk_ref[0:T, :] = ksn_ref[0].astype(BF16)
    newv_ref[0:T, :] = vsn_ref[0].astype(BF16)
    onehot = _block_onehot(NSB_S_PAD, NEW_PAD, PAST_LEN // SEL_BLOCK)
    s = jnp.where(new_ok, _dot_nt(q, newk_ref[...]) + _dot(a, onehot), NEG)
    _online_update(s, newv_ref[...], m_ref, l_ref, acc_ref, allrows)
    o_s = acc_ref[...] / _lane_tile(l_ref[...], KV_W // LANES)

    wpos = (PAST_LEN - WINDOW) + lax.broadcasted_iota(jnp.int32, (1, WINDOW), 1)
    newk_ref[0:T, :] = kwn_ref[0].astype(BF16)
    newv_ref[0:T, :] = vwn_ref[0].astype(BF16)
    s3 = jnp.concatenate(
        [jnp.where(_window_mask(tcol, wpos), _dot(q, kwc_ref[0, 0].astype(BF16)), NEG),
         jnp.where(new_ok, _dot_nt(q, newk_ref[...]), NEG)], axis=1)
    e3 = jnp.exp(s3 - jnp.max(s3, axis=1, keepdims=True))
    p3 = e3.astype(BF16)
    o_w = (_dot_nt(p3[:, 0:WINDOW], vwc_ref[0, 0].astype(BF16)) + _dot(p3[:, WINDOW:], newv_ref[...]))
    o_w = o_w / jnp.sum(e3, axis=1, keepdims=True)

    sig = _sigmoid(gt_ref[0])
    comb = jnp.where(diag, sig[:, 0:1] * o_c + sig[:, 1:2] * o_s + sig[:, 2:3] * o_w, 0.0)
    folded = comb[:, 0:LANES] + comb[:, LANES:2 * LANES]
    folded = folded + pltpu.roll(folded, HEAD_DIM, 1)
    za = za_ref[0]
    o_ref[0] = folded[:, 0:HEAD_DIM] * (za * _sigmoid(za))


def _sattn(page_table, qt, ck, cv, mmat, cache_k, cache_v, ksn, vsn, win_k, win_v, kwn, vwn, gates, za, layer):
    per_b = lambda shape: pl.BlockSpec((1,) + shape, lambda b, pt: (b,) + (0,) * len(shape))
    win = pl.BlockSpec((1, 1, KV_W, WINDOW), lambda b, pt: (layer, b, 0, 0))
    hbm = pl.BlockSpec(memory_space=pl.ANY)
    return pl.pallas_call(
        functools.partial(_sattn_kernel, layer=layer),
        grid_spec=pltpu.PrefetchScalarGridSpec(
            num_scalar_prefetch=1, grid=(DEC_BATCH,),
            in_specs=[per_b((S_ROWS, KV_W)), per_b((N_CMP, KV_W)), per_b((N_CMP, KV_W)),
                      pl.BlockSpec((N_CMP, NSB_S_PAD), lambda b, pt: (0, 0)),
                      hbm, hbm, per_b((DEC_SEQ, KV_W)), per_b((DEC_SEQ, KV_W)),
                      win, win, per_b((DEC_SEQ, KV_W)), per_b((DEC_SEQ, KV_W)),
                      per_b((S_ROWS, LANES)), per_b((S_ROWS, HEAD_DIM))],
            out_specs=per_b((S_ROWS, HEAD_DIM)),
            scratch_shapes=[pltpu.VMEM((2, KV_W, SLC_CHUNK_ROWS), F32), pltpu.VMEM((2, KV_W, SLC_CHUNK_ROWS), F32),
                            pltpu.SemaphoreType.DMA((2,)), pltpu.SemaphoreType.DMA((2,)),
                            pltpu.VMEM((NEW_PAD, KV_W), BF16), pltpu.VMEM((NEW_PAD, KV_W), BF16),
                            pltpu.VMEM((S_ROWS, NSB_S_PAD), BF16), pltpu.VMEM((S_ROWS, KV_W), F32),
                            pltpu.VMEM((S_ROWS, LANES), F32), pltpu.VMEM((S_ROWS, LANES), F32)]),
        out_shape=jax.ShapeDtypeStruct((DEC_BATCH, S_ROWS, HEAD_DIM), F32),
        compiler_params=pltpu.CompilerParams(dimension_semantics=("arbitrary",), vmem_limit_bytes=VMEM_LIMIT),
        name="attn_sample",
    )(page_table, qt, ck, cv, mmat, cache_k, cache_v, ksn, vsn, win_k, win_v, kwn, vwn, gates, za)


OUT_TM = 256


def _outproj_kernel(h_ref, yc_ref, ya_ref, pe_ref, wc_ref, wa_ref, wg_ref, wp_ref, fg_ref, o_ref, *, final):
    h1 = h_ref[...] + (_dot(yc_ref[...], wc_ref[...]) + _dot(ya_ref[...], wa_ref[...]))
    h2 = h1 + _sigmoid(_dot(h1.astype(BF16), wg_ref[...])) * _dot(pe_ref[...].astype(BF16), wp_ref[...])
    if final:
        h2 = (h2 * lax.rsqrt(jnp.mean(h2 * h2, axis=-1, keepdims=True) + EPS)) * fg_ref[...]
    o_ref[...] = h2


def _outproj(h, yc, ya, pe, wc, wa, wg, wp, fg, final):
    row = lambda w: pl.BlockSpec((OUT_TM, w), lambda i: (i, 0))
    const = lambda shape: pl.BlockSpec(shape, lambda i: (0, 0), pipeline_mode=pl.Buffered(1))
    return pl.pallas_call(
        functools.partial(_outproj_kernel, final=final),
        grid=(N_ROWS // OUT_TM,),
        in_specs=[row(D_MODEL), row(C_CONV), row(ATTN_W), row(PLE_DIM),
                  const((C_CONV, D_MODEL)), const((ATTN_W, D_MODEL)), const((D_MODEL, D_MODEL)),
                  const((PLE_DIM, D_MODEL)), const((1, D_MODEL))],
        out_specs=row(D_MODEL),
        out_shape=jax.ShapeDtypeStruct((N_ROWS, D_MODEL), F32),
        compiler_params=pltpu.CompilerParams(dimension_semantics=("parallel",), vmem_limit_bytes=VMEM_LIMIT),
        name="outproj_final" if final else "outproj",
    )(h, yc, ya, pe, wc, wa, wg, wp, fg)


def _rope_tables():
    pos = jnp.concatenate([jnp.arange(SEQ), PAST_LEN + jnp.tile(jnp.arange(DEC_SEQ), DEC_BATCH)])
    half = ROT_DIM // 2
    inv = ROPE_THETA ** (-2.0 * jnp.arange(half, dtype=F32) / ROT_DIM)
    ang = pos.astype(F32)[:, None] * inv[None, :]
    cos, sin = jnp.cos(ang), jnp.sin(ang)
    one = jnp.ones((N_ROWS, HEAD_DIM - ROT_DIM), F32)
    zero = lambda n: jnp.zeros((N_ROWS, n), F32)
    c = jnp.concatenate([cos, cos, one], axis=1)
    sa = jnp.concatenate([-sin, zero(HEAD_DIM - half)], axis=1)
    sb = jnp.concatenate([zero(half), sin, zero(HEAD_DIM - ROT_DIM)], axis=1)
    return tuple(jnp.concatenate([t, t], axis=1) for t in (c, sa, sb))


def _cmp_to_sel(nsb, pad):
    n = jnp.arange(N_CMP)[:, None] * CMP_STRIDE
    j = jnp.arange(pad)[None, :]
    m = (n < (j + 1) * SEL_BLOCK) & (n + CMP_LEN > j * SEL_BLOCK) & (j < nsb)
    return m.astype(BF16)


def _expand_cmp_weight(w):
    w3 = w.reshape(CMP_LEN, HEAD_DIM, HEAD_DIM)
    eye = jnp.eye(KV_HEADS, dtype=w.dtype)
    return jnp.einsum('lde,gh->lgdhe', w3, eye).reshape(CMP_LEN, KV_W, KV_W).astype(BF16)


def _pages_transposed(cache):
    d, n, p = cache.shape[:3]
    return cache.transpose(0, 1, 3, 4, 2).reshape(d, n, KV_W, p)


def _to_head_rows(x, width):
    return x.reshape(DEC_BATCH, DEC_SEQ, N_HEADS, width).transpose(0, 2, 1, 3).reshape(DEC_BATCH, S_ROWS, width)


def kernel(x_prompt, x_sample, cache_cmp_k, cache_cmp_v, cache_slc_k, cache_slc_v, cache_win_k, cache_win_v,
           state_conv, page_table, p_prompt, p_sample, norm_g, w_in, conv_w, conv_b, conv_ln_g, conv_ln_b,
           w_cmp_k, w_cmp_v, w_out, w_ple, w_ple_gate, final_g):
    h = jnp.concatenate([x_prompt.reshape(SEQ, D_MODEL), x_sample.reshape(N_SAMPLE, D_MODEL)], axis=0)
    cos, sa, sb = _rope_tables()
    mm_p = _cmp_to_sel(NSB_P, NSB_P)
    mm_s = _cmp_to_sel(NSB_S, NSB_S_PAD)
    cck, ccv = _pages_transposed(cache_cmp_k), _pages_transposed(cache_cmp_v)
    csk, csv = _pages_transposed(cache_slc_k), _pages_transposed(cache_slc_v)
    cwk, cwv = _pages_transposed(cache_win_k), _pages_transposed(cache_win_v)
    hist_p = jnp.zeros((1, HIST_PAD, C_CONV), F32)
    hist_s = jnp.pad(state_conv, ((0, 0), (0, 0), (HIST_OFF, 0), (0, 0)))

    o_q = 3 * C_CONV
    o_kv = o_q + ATTN_W
    o_gt = o_kv + 6 * KV_W
    o_za = o_gt + 3 * N_HEADS
    kv = lambda wi, n: wi[:, o_kv + n * KV_W:o_kv + (n + 1) * KV_W]

    st_p, st_s = [], []
    for i in range(DEPTH):
        wi = w_in[i]
        w_a = jnp.concatenate([wi[:, :o_q], wi[:, o_za:]], axis=1).astype(BF16)
        w_b = jnp.concatenate([wi[:, o_q:o_kv], kv(wi, 0), kv(wi, 2), kv(wi, 4), kv(wi, 1), kv(wi, 3), kv(wi, 5)],
                              axis=1).astype(BF16)
        w_g = jnp.pad(wi[:, o_gt:o_za], ((0, 0), (0, LANES - 3 * N_HEADS))).astype(BF16)
        g_row = norm_g[i].reshape(1, D_MODEL)

        ua, = _inproj(h, g_row, w_a)
        ub, ubb, gates = _inproj(h, g_row, w_b, extras=(w_g, cos, sa, sb))

        conv_args = (conv_w[i], conv_b[i].reshape(1, C_CONV), conv_ln_g[i].reshape(1, C_CONV),
                     conv_ln_b[i].reshape(1, C_CONV))
        yc_p, nc_p = _conv(ua, hist_p, *conv_args, row0=0, batch=1, t_len=SEQ, tm=256)
        yc_s, nc_s = _conv(ua, hist_s[i], *conv_args, row0=SEQ, batch=DEC_BATCH, t_len=DEC_SEQ, tm=DEC_SEQ)

        seg = lambda n: ub[:, ATTN_W + n * KV_W:ATTN_W + (n + 1) * KV_W]
        kc, ks, kw, vc, vs, vw = (seg(n) for n in range(6))
        wab_k, wab_v = _expand_cmp_weight(w_cmp_k[i]), _expand_cmp_weight(w_cmp_v[i])

        ck_p = _pcompress(ub, B_KC, wab_k)
        cv_p = _pcompress(ub, B_VC, wab_v)
        ya_p = _pattn(ub, ck_p, cv_p, mm_p, _pack_keys(ub, B_KS), ubb, gates, ua)

        new_rows = lambda x: x[SEQ:].reshape(DEC_BATCH, DEC_SEQ, KV_W)
        ck_s = _scompress(page_table, cck, new_rows(kc), wab_k, i)
        cv_s = _scompress(page_table, ccv, new_rows(vc), wab_v, i)
        qt = jnp.tile(_to_head_rows(ub[SEQ:, :ATTN_W], HEAD_DIM), (1, 1, KV_HEADS))
        gt_s = gates[SEQ:, :3 * N_HEADS].reshape(DEC_BATCH, DEC_SEQ, 3, N_HEADS).transpose(0, 3, 1, 2)
        gt_s = jnp.pad(gt_s.reshape(DEC_BATCH, S_ROWS, 3), ((0, 0), (0, 0), (0, LANES - 3)))
        za_s = _to_head_rows(ua[SEQ:, 3 * C_CONV:], HEAD_DIM)
        ya_s = _sattn(page_table, qt, ck_s, cv_s, mm_s, csk, csv, new_rows(ks), new_rows(vs),
                      cwk, cwv, new_rows(kw), new_rows(vw), gt_s, za_s, i)
        ya_s = ya_s.reshape(DEC_BATCH, N_HEADS, DEC_SEQ, HEAD_DIM).transpose(0, 2, 1, 3).reshape(N_SAMPLE, ATTN_W)

        yc = jnp.concatenate([yc_p, yc_s.astype(BF16)], axis=0)
        ya = jnp.concatenate([ya_p, ya_s.astype(BF16)], axis=0)
        pe = jnp.concatenate([p_prompt[i].reshape(SEQ, PLE_DIM), p_sample[i].reshape(N_SAMPLE, PLE_DIM)], axis=0)
        wo = w_out[i].astype(BF16)
        h = _outproj(h, yc, ya, pe, wo[:C_CONV], wo[C_CONV:], w_ple_gate[i].astype(BF16), w_ple[i].astype(BF16),
                     final_g.reshape(1, D_MODEL), final=(i == DEPTH - 1))

        heads = lambda x, b, t: x.reshape(b, t, KV_HEADS, HEAD_DIM)
        wb = min(WINDOW, SEQ)
        st_p.append((heads(kc[:SEQ], 1, SEQ), heads(vc[:SEQ], 1, SEQ), heads(ks[:SEQ], 1, SEQ), heads(vs[:SEQ], 1, SEQ),
                     heads(kw[SEQ - wb:SEQ], 1, wb), heads(vw[SEQ - wb:SEQ], 1, wb), nc_p[:, HIST_OFF:]))
        hs = lambda x: heads(x[SEQ:], DEC_BATCH, DEC_SEQ)
        st_s.append((hs(kc), hs(vc), hs(ks), hs(vs),
                     jnp.concatenate([cache_win_k[i][:, DEC_SEQ:], hs(kw)], axis=1),
                     jnp.concatenate([cache_win_v[i][:, DEC_SEQ:], hs(vw)], axis=1), nc_s[:, HIST_OFF:]))

    y_prompt = h[:SEQ].reshape(1, SEQ, D_MODEL)
    y_sample = h[SEQ:].reshape(DEC_BATCH, DEC_SEQ, D_MODEL)
    outs_p = [jnp.stack([s[n] for s in st_p]) for n in range(7)]
    outs_s = [jnp.stack([s[n] for s in st_s]) for n in range(7)]
    return (y_prompt, y_sample, *outs_p, *outs_s)
```

```python
import functools

import jax
import jax.numpy as jnp
from jax import lax
from jax.experimental import pallas as pl
from jax.experimental.pallas import tpu as pltpu

F32 = jnp.float32
BF16 = jnp.bfloat16

D_MODEL = 2048
SEQ = 8192
DEPTH = 2
DEC_BATCH = 32
DEC_SEQ = 8
PAST_LEN = 8192
PAGE_SIZE = 128
HEAD_DIM = 64
ATTN_W = D_MODEL // 2
C_CONV = D_MODEL - ATTN_W
N_HEADS = ATTN_W // HEAD_DIM
KV_HEADS = N_HEADS // 4
GQA = N_HEADS // KV_HEADS
KV_W = KV_HEADS * HEAD_DIM
ROT_DIM = HEAD_DIM // 4
ROPE_THETA = 500000.0
CONV_WIDTH = 31
CMP_LEN = 32
CMP_STRIDE = 16
SEL_BLOCK = 64
SEL_TOPK = 16
WINDOW = 512
Q_BLOCK = 128
PLE_DIM = 256
EPS = 1e-6
NEG = -1e30
BIG = 1e9

N_SAMPLE = DEC_BATCH * DEC_SEQ
N_ROWS = SEQ + N_SAMPLE
N_PAGES = PAST_LEN // PAGE_SIZE
N_CHUNK = PAST_LEN // CMP_STRIDE
N_CMP = N_CHUNK
CHUNK_W = CMP_STRIDE * KV_W
PAGE_CHUNKS = PAGE_SIZE // CMP_STRIDE
NSB_P = SEQ // SEL_BLOCK
NSB_S = -(-(PAST_LEN + DEC_SEQ) // SEL_BLOCK)
NSB_S_PAD = 256
BELOW_NEG = -3e38
LOG2E = 1.4426950408889634
SCALE = HEAD_DIM ** -0.5 * LOG2E

LANES = 128
SUBLANES = 8
VMEM_LIMIT = 56 * 1024 * 1024

A_COLS = 4 * C_CONV
B_COLS = ATTN_W + 6 * KV_W
B_ROPE_COLS = ATTN_W + 3 * KV_W
B_KC, B_KS, B_KW, B_VC, B_VS, B_VW = (ATTN_W // KV_W + n for n in range(6))
ROW_TILE = N_ROWS // 8
PROJ_TN = 512
SLC_KT = 512


def _dot(a, b):
    return jnp.dot(a, b, preferred_element_type=F32)


def _dot_nt(a, b):
    return lax.dot_general(a, b, (((1,), (1,)), ((), ())), preferred_element_type=F32)


def _lane_tile(x, n):
    return x if n == 1 else jnp.concatenate([x] * n, axis=1)


def _sigmoid(x):
    return 1.0 / (1.0 + jnp.exp(-x))


def _inproj_kernel(*refs, rope_chunks, extras):
    if extras:
        x_ref, g_ref, w_ref, wg_ref, cos_ref, sa_ref, sb_ref, u_ref, ub_ref, gate_ref, hn_ref = refs
    else:
        x_ref, g_ref, w_ref, u_ref, hn_ref = refs
    j = pl.program_id(1)

    @pl.when(j == 0)
    def _():
        x = x_ref[...]
        r = lax.rsqrt(jnp.mean(x * x, axis=-1, keepdims=True) + EPS)
        hn = ((x * r) * g_ref[...]).astype(BF16)
        hn_ref[...] = hn
        if extras:
            gate_ref[...] = _dot(hn, wg_ref[...])

    acc = _dot(hn_ref[...], w_ref[...])
    if not extras:
        u_ref[...] = acc
        return

    def store(n_rope):
        c, sa, sb = cos_ref[...], sa_ref[...], sb_ref[...]
        for k in range(acc.shape[1] // LANES):
            cs = slice(k * LANES, (k + 1) * LANES)
            y = acc[:, cs]
            if k < n_rope:
                y = y * c + pltpu.roll(y, LANES - ROT_DIM // 2, 1) * sa + pltpu.roll(y, ROT_DIM // 2, 1) * sb
            u_ref[:, cs] = y
            ub_ref[:, cs] = y.astype(BF16)

    full_tiles, part_chunks = rope_chunks // (acc.shape[1] // LANES), rope_chunks % (acc.shape[1] // LANES)
    pl.when(j < full_tiles)(lambda: store(acc.shape[1] // LANES))
    if part_chunks:
        pl.when(j == full_tiles)(lambda: store(part_chunks))
    pl.when(j >= full_tiles + bool(part_chunks))(lambda: store(0))


def _inproj(x, g, w, extras=None):
    n_cols = w.shape[1]
    tn = PROJ_TN
    grid = (N_ROWS // ROW_TILE, n_cols // tn)
    row = lambda i, j: (i, 0)
    in_specs = [pl.BlockSpec((ROW_TILE, D_MODEL), row),
                pl.BlockSpec((1, D_MODEL), lambda i, j: (0, 0)),
                pl.BlockSpec((D_MODEL, tn), lambda i, j: (0, j))]
    out_specs = [pl.BlockSpec((ROW_TILE, tn), lambda i, j: (i, j))]
    out_shape = [jax.ShapeDtypeStruct((N_ROWS, n_cols), F32)]
    args = [x, g, w]
    if extras:
        wg, cos, sa, sb = extras
        in_specs += [pl.BlockSpec((D_MODEL, LANES), lambda i, j: (0, 0))] + [pl.BlockSpec((ROW_TILE, LANES), row)] * 3
        out_specs += [pl.BlockSpec((ROW_TILE, tn), lambda i, j: (i, j)), pl.BlockSpec((ROW_TILE, LANES), row)]
        out_shape += [jax.ShapeDtypeStruct((N_ROWS, n_cols), BF16), jax.ShapeDtypeStruct((N_ROWS, LANES), F32)]
        args += [wg, cos, sa, sb]
    return pl.pallas_call(
        functools.partial(_inproj_kernel, rope_chunks=B_ROPE_COLS // LANES, extras=bool(extras)),
        grid=grid, in_specs=in_specs, out_specs=out_specs, out_shape=out_shape,
        scratch_shapes=[pltpu.VMEM((ROW_TILE, D_MODEL), BF16)],
        compiler_params=pltpu.CompilerParams(dimension_semantics=("parallel", "arbitrary"),
                                             vmem_limit_bytes=VMEM_LIMIT),
        name="inproj_b" if extras else "inproj_a",
    )(*args)


HIST_PAD = 32
HIST_OFF = HIST_PAD - (CONV_WIDTH - 1)


def _conv_kernel(a_ref, b_ref, zc_ref, hist_ref, w_ref, cb_ref, lg_ref, lb_ref, yc_ref, nc_ref, ext_ref, y_ref, *, tm):
    t = pl.program_id(1)

    @pl.when(t == 0)
    def _():
        ext_ref[0:HIST_PAD, :] = hist_ref[0]

    ext_ref[HIST_PAD:HIST_PAD + tm, :] = a_ref[...] * _sigmoid(b_ref[...])
    for c in range(C_CONV // LANES):
        cs = slice(c * LANES, (c + 1) * LANES)
        acc = jnp.zeros((tm, LANES), F32) + cb_ref[:, cs]
        for k in range(CONV_WIDTH):
            acc = acc + ext_ref[HIST_OFF + k:HIST_OFF + k + tm, cs] * w_ref[k:k + 1, cs]
        y_ref[:, cs] = acc
    y = y_ref[...]
    mu = jnp.mean(y, axis=-1, keepdims=True)
    yc = y - mu
    yn = yc * lax.rsqrt(jnp.mean(yc * yc, axis=-1, keepdims=True) + EPS) * lg_ref[...] + lb_ref[...]
    zc = zc_ref[...]
    yc_ref[...] = ((yn * _sigmoid(yn)) * (zc * _sigmoid(zc))).astype(yc_ref.dtype)
    tail = ext_ref[tm:tm + HIST_PAD, :]
    ext_ref[0:HIST_PAD, :] = tail

    @pl.when(t == pl.num_programs(1) - 1)
    def _():
        nc_ref[0] = tail


def _conv(ua, hist, w, cb, lg, lb, *, row0, batch, t_len, tm):
    nt = t_len // tm
    blk0 = row0 // tm
    col = lambda c: (lambda b, t: (blk0 + b * nt + t, c))
    vec = pl.BlockSpec((1, C_CONV), lambda b, t: (0, 0))
    return pl.pallas_call(
        functools.partial(_conv_kernel, tm=tm),
        grid=(batch, nt),
        in_specs=[pl.BlockSpec((tm, C_CONV), col(0)), pl.BlockSpec((tm, C_CONV), col(1)),
                  pl.BlockSpec((tm, C_CONV), col(2)),
                  pl.BlockSpec((1, HIST_PAD, C_CONV), lambda b, t: (b, 0, 0)),
                  pl.BlockSpec((CONV_WIDTH, C_CONV), lambda b, t: (0, 0)), vec, vec, vec],
        out_specs=[pl.BlockSpec((tm, C_CONV), lambda b, t: (b * nt + t, 0)),
                   pl.BlockSpec((1, HIST_PAD, C_CONV), lambda b, t: (b, 0, 0))],
        out_shape=[jax.ShapeDtypeStruct((batch * t_len, C_CONV), BF16 if tm % 16 == 0 else F32),
                   jax.ShapeDtypeStruct((batch, HIST_PAD, C_CONV), F32)],
        scratch_shapes=[pltpu.VMEM((HIST_PAD + tm, C_CONV), F32), pltpu.VMEM((tm, C_CONV), F32)],
        compiler_params=pltpu.CompilerParams(dimension_semantics=("parallel", "arbitrary"),
                                             vmem_limit_bytes=VMEM_LIMIT),
        name="conv_group",
    )(ua, ua, ua, hist, w, cb, lg, lb)


CMP_ROWS = PAST_LEN + CMP_STRIDE


LANE_HALVES = KV_W // LANES


def _compress_rows(x_ref, w_ref):
    acc = None
    for l in range(CMP_LEN):
        xl = jnp.concatenate([x_ref[c, pl.ds(l, N_CMP, stride=CMP_STRIDE), :] for c in range(LANE_HALVES)], axis=1)
        part = _dot(xl.astype(BF16), w_ref[l])
        acc = part if acc is None else acc + part
    return acc


def _pcompress_kernel(x_ref, w_ref, o_ref, rows_ref):
    for c in range(LANE_HALVES):
        rows_ref[c, 0:SEQ, :] = x_ref[:, c * LANES:(c + 1) * LANES]
        rows_ref[c, SEQ:, :] = jnp.zeros((CMP_ROWS - SEQ, LANES), F32)
    o_ref[...] = _compress_rows(rows_ref, w_ref).astype(o_ref.dtype)


def _pcompress(ub, col_block, w):
    once = pl.Buffered(1)
    return pl.pallas_call(
        _pcompress_kernel,
        grid=(1,),
        in_specs=[pl.BlockSpec((SEQ, KV_W), lambda i: (0, col_block), pipeline_mode=once),
                  pl.BlockSpec((CMP_LEN, KV_W, KV_W), lambda i: (0, 0, 0), pipeline_mode=once)],
        out_specs=pl.BlockSpec((N_CMP, KV_W), lambda i: (0, 0)),
        out_shape=jax.ShapeDtypeStruct((N_CMP, KV_W), BF16),
        scratch_shapes=[pltpu.VMEM((LANE_HALVES, CMP_ROWS, LANES), F32)],
        compiler_params=pltpu.CompilerParams(dimension_semantics=("arbitrary",), vmem_limit_bytes=VMEM_LIMIT),
        name="compress_prompt",
    )(ub, w)


S_CHUNKS = N_CHUNK + SUBLANES


def _scompress_kernel(pt_ref, cache_ref, new_ref, perm_ref, wab_ref, o_ref, tbuf, xc_ref, sem, *, layer):
    b = pl.program_id(0)
    slot = b % 2

    def copies(bb, sl, go):
        def body(p, carry):
            cp = pltpu.make_async_copy(cache_ref.at[layer, pt_ref[bb, p]],
                                       tbuf.at[sl, :, pl.ds(pl.multiple_of(p * PAGE_SIZE, PAGE_SIZE), PAGE_SIZE)],
                                       sem.at[sl])
            cp.start() if go else cp.wait()
            return carry
        lax.fori_loop(0, N_PAGES, body, 0)

    @pl.when(b == 0)
    def _():
        copies(0, 0, True)

    @pl.when(b + 1 < pl.num_programs(0))
    def _():
        copies(b + 1, 1 - slot, True)

    copies(b, slot, False)
    perm = perm_ref[...]
    for pp in range(N_PAGES // 2):
        pages = tbuf[slot, :, pp * 2 * PAGE_SIZE:(pp + 1) * 2 * PAGE_SIZE].T.astype(BF16)
        rows = _dot(perm, pages)
        for k in range(2):
            p = 2 * pp + k
            for l in range(CMP_STRIDE):
                r0 = k * PAGE_SIZE + l * PAGE_CHUNKS
                xc_ref[p * PAGE_CHUNKS:(p + 1) * PAGE_CHUNKS, l * KV_W:(l + 1) * KV_W] = rows[r0:r0 + PAGE_CHUNKS, :]
    xc_ref[N_CHUNK:, :] = jnp.zeros((SUBLANES, CHUNK_W), F32)
    for l in range(DEC_SEQ):
        xc_ref[N_CHUNK:N_CHUNK + 1, l * KV_W:(l + 1) * KV_W] = new_ref[0, l:l + 1, :]
    z = _dot(xc_ref[...].astype(BF16), wab_ref[...])
    zb = pltpu.roll(z[:, KV_W:], S_CHUNKS - 1, 0)
    o_ref[0] = (z[0:N_CMP, :KV_W] + zb[0:N_CMP]).astype(o_ref.dtype)


def _scompress(page_table, cache_t, new_rows, perm, wab, layer):
    once = pl.Buffered(1)
    return pl.pallas_call(
        functools.partial(_scompress_kernel, layer=layer),
        grid_spec=pltpu.PrefetchScalarGridSpec(
            num_scalar_prefetch=1, grid=(DEC_BATCH,),
            in_specs=[pl.BlockSpec(memory_space=pl.ANY),
                      pl.BlockSpec((1, DEC_SEQ, KV_W), lambda b, pt: (b, 0, 0)),
                      pl.BlockSpec((2 * PAGE_SIZE, 2 * PAGE_SIZE), lambda b, pt: (0, 0), pipeline_mode=once),
                      pl.BlockSpec((CHUNK_W, 2 * KV_W), lambda b, pt: (0, 0), pipeline_mode=once)],
            out_specs=pl.BlockSpec((1, N_CMP, KV_W), lambda b, pt: (b, 0, 0)),
            scratch_shapes=[pltpu.VMEM((2, KV_W, PAST_LEN), F32), pltpu.VMEM((S_CHUNKS, CHUNK_W), F32),
                            pltpu.SemaphoreType.DMA((2,))]),
        out_shape=jax.ShapeDtypeStruct((DEC_BATCH, N_CMP, KV_W), BF16),
        compiler_params=pltpu.CompilerParams(dimension_semantics=("arbitrary",), vmem_limit_bytes=VMEM_LIMIT),
        name="compress_sample",
    )(page_table, cache_t, new_rows, perm, wab)


def _cmp_branch(q, tcol, ck, cv):
    s = _dot_nt(q, ck)
    n = lax.broadcasted_iota(jnp.int32, s.shape, 1)
    vis = (n * CMP_STRIDE + (CMP_LEN - 1)) <= tcol
    sm = jnp.where(vis, s, NEG)
    e = jnp.where(vis, jnp.exp2(sm - jnp.max(sm, axis=1, keepdims=True)), 0.0)
    den = jnp.sum(e, axis=1, keepdims=True)
    p = e / jnp.where(den > 0.0, den, 1.0)
    return _dot(p.astype(BF16), cv), p


def _select_bias(imp, t, nsb, axis):
    j = lax.broadcasted_iota(jnp.int32, imp.shape, axis)
    cur = lax.shift_right_arithmetic(t, 6)
    forced = (j == 0) | (j == cur) | (j == cur - 1)
    score = jnp.where(j * SEL_BLOCK <= t, jnp.where(forced, BIG, imp), NEG)
    score = jnp.where(j < nsb, score, BELOW_NEG)
    jf = j.astype(F32)
    bias = jnp.full(imp.shape, NEG, F32)
    for _ in range(SEL_TOPK):
        mx = jnp.max(score, axis=axis, keepdims=True)
        idx = jnp.min(jnp.where(score == mx, jf, 1e9), axis=axis, keepdims=True)
        pick = jf == idx
        bias = jnp.where(pick, 0.0, bias)
        score = jnp.where(pick, BELOW_NEG, score)
    return bias


def _block_onehot(nb, kt, first_block):
    j = lax.broadcasted_iota(jnp.int32, (nb, kt), 0)
    col = lax.broadcasted_iota(jnp.int32, (nb, kt), 1)
    return jnp.where(j - lax.shift_right_arithmetic(col, 6) == first_block, 1.0, 0.0).astype(BF16)


def _online_update(s, v, m_ref, l_ref, acc_ref, rows, v_transposed=False):
    m_prev, l_prev = m_ref[rows, :], l_ref[rows, :]
    m_next = jnp.maximum(m_prev, jnp.max(s, axis=1, keepdims=True))
    alpha = jnp.exp2(m_prev - m_next)
    p = jnp.exp2(s - _lane_tile(m_next, s.shape[1] // LANES))
    l_ref[rows, :] = alpha * l_prev + jnp.sum(p, axis=1, keepdims=True)
    m_ref[rows, :] = m_next
    pv = _dot_nt(p.astype(BF16), v) if v_transposed else _dot(p.astype(BF16), v)
    acc_ref[rows, :] = acc_ref[rows, :] * _lane_tile(alpha, KV_W // LANES) + pv


def _window_mask(tcol, kpos):
    return lax.bitcast_convert_type(tcol - kpos, jnp.uint32) <= jnp.uint32(WINDOW)


GROUP_ROWS = GQA * Q_BLOCK


PACK_TM = 512


def _pack_keys_kernel(k_ref, o_ref):
    i = pl.program_id(0)
    lane = lax.broadcasted_iota(jnp.int32, (PACK_TM, LANES), 1)
    blk = lax.shift_right_arithmetic(i * PACK_TM + lax.broadcasted_iota(jnp.int32, (PACK_TM, LANES), 0), 6)
    lo = lane < HEAD_DIM
    hot_lo = jnp.where(lane - HEAD_DIM == blk, 1.0, 0.0)
    hot_hi = jnp.where(lane + HEAD_DIM == blk, 1.0, 0.0)
    for g in range(KV_HEADS):
        pair = k_ref[:, (g // 2) * LANES:(g // 2 + 1) * LANES]
        if g % 2:
            pair = pltpu.roll(pair, HEAD_DIM, 1)
        o_ref[g, :, 0:LANES] = jnp.where(lo, pair, hot_lo).astype(BF16)
        o_ref[g, :, LANES:2 * LANES] = jnp.where(lo, hot_hi, 0.0).astype(BF16)


def _pack_keys(ub, col_block):
    assert NSB_P == 2 * HEAD_DIM and KV_W == 2 * LANES
    return pl.pallas_call(
        _pack_keys_kernel,
        grid=(SEQ // PACK_TM,),
        in_specs=[pl.BlockSpec((PACK_TM, KV_W), lambda i: (i, col_block))],
        out_specs=pl.BlockSpec((KV_HEADS, PACK_TM, KV_W), lambda i: (0, i, 0)),
        out_shape=jax.ShapeDtypeStruct((KV_HEADS, SEQ, KV_W), BF16),
        compiler_params=pltpu.CompilerParams(dimension_semantics=("parallel",), vmem_limit_bytes=VMEM_LIMIT),
        name="pack_keys",
    )(ub)


def _pattn_kernel(q_ref, ck_ref, cv_ref, mm_ref, ka_ref, vs_ref, kw_ref, vw_ref, gt_ref, za_ref, o_ref,
                  qbd_ref, qa_ref, oc_ref, acc_ref, m_ref, l_ref):
    i = pl.program_id(0)
    q0 = i * Q_BLOCK
    T, R = Q_BLOCK, GROUP_ROWS
    lane = lax.broadcasted_iota(jnp.int32, (T, LANES), 1)
    half = lax.shift_right_arithmetic(lane, 6)
    lo = lane < HEAD_DIM
    tcol = q0 + (lax.broadcasted_iota(jnp.int32, (R, 1), 0) & (T - 1))
    trow = q0 + lax.broadcasted_iota(jnp.int32, (1, T), 1)

    zeros = jnp.zeros((T, LANES), BF16)
    for hh in range(N_HEADS):
        g = hh // GQA
        pair = q_ref[:, (hh // 2) * LANES:(hh // 2 + 1) * LANES] * SCALE
        if hh % 2 != g % 2:
            pair = pltpu.roll(pair, HEAD_DIM, 1)
        x = jnp.where(half == g % 2, pair, 0.0).astype(BF16)
        qbd_ref[hh * T:(hh + 1) * T, 0:LANES] = x if g // 2 == 0 else zeros
        qbd_ref[hh * T:(hh + 1) * T, LANES:2 * LANES] = zeros if g // 2 == 0 else x

    for g in range(KV_HEADS):
        rows = slice(g * R, (g + 1) * R)
        o_c, p = _cmp_branch(qbd_ref[rows, :], tcol, ck_ref[...], cv_ref[...])
        oc_ref[rows, :] = o_c
        imp_rows = _dot(p.astype(BF16), mm_ref[...])
        imp = imp_rows[0:T] + imp_rows[T:2 * T] + imp_rows[2 * T:3 * T] + imp_rows[3 * T:4 * T]
        bias = _select_bias(imp.T, trow, NSB_P, axis=0).T
        bias = pltpu.roll(bias, HEAD_DIM, 1)
        for h in range(GQA):
            hh = g * GQA + h
            pair = q_ref[:, (hh // 2) * LANES:(hh // 2 + 1) * LANES] * SCALE
            if hh % 2:
                pair = pltpu.roll(pair, HEAD_DIM, 1)
            qa_ref[hh * T:(hh + 1) * T, 0:LANES] = jnp.where(lo, pair, bias).astype(BF16)
            qa_ref[hh * T:(hh + 1) * T, LANES:2 * LANES] = jnp.where(lo, bias, 0.0).astype(BF16)

    m_ref[...] = jnp.full(m_ref.shape, NEG, F32)
    l_ref[...] = jnp.zeros(l_ref.shape, F32)
    acc_ref[...] = jnp.zeros(acc_ref.shape, F32)
    kcol = lax.broadcasted_iota(jnp.int32, (1, SLC_KT), 1)

    def slc_tile(c, causal):
        k0 = pl.multiple_of(c * SLC_KT, SLC_KT)
        v_t = vs_ref[pl.ds(k0, SLC_KT), :]
        for g in range(KV_HEADS):
            rows = slice(g * R, (g + 1) * R)
            s = _dot_nt(qa_ref[rows, :], ka_ref[g, pl.ds(k0, SLC_KT), :])
            if causal:
                s = jnp.where(k0 + kcol <= tcol, s, NEG)
            _online_update(s, v_t, m_ref, l_ref, acc_ref, rows)

    c_last = (q0 + T - 1) // SLC_KT

    def full_tile(c, carry):
        slc_tile(c, False)
        return carry

    lax.fori_loop(0, c_last, full_tile, 0)
    slc_tile(c_last, True)

    start = pl.multiple_of(jnp.maximum(q0 - WINDOW, 0), Q_BLOCK)
    kw_t = kw_ref[pl.ds(start, WINDOW + T), :]
    vw_t = vw_ref[pl.ds(start, WINDOW + T), :]
    kpos = start + lax.broadcasted_iota(jnp.int32, (1, WINDOW + T), 1)
    sig = _sigmoid(gt_ref[...])
    for g in range(KV_HEADS):
        rows = slice(g * R, (g + 1) * R)
        cc = slice((g // 2) * LANES, (g // 2 + 1) * LANES)
        s3 = jnp.where(_window_mask(tcol, kpos), _dot_nt(qbd_ref[rows, :], kw_t), NEG)
        e3 = jnp.exp2(s3 - jnp.max(s3, axis=1, keepdims=True))
        o_w = _dot(e3.astype(BF16), vw_t)[:, cc] / jnp.sum(e3, axis=1, keepdims=True)
        o_s = acc_ref[rows, cc] / l_ref[rows, :]
        o_c = oc_ref[rows, cc]
        for pair in range(2):
            pieces = []
            for sub in range(2):
                h = 2 * pair + sub
                hh = g * GQA + h
                hr = slice(h * T, (h + 1) * T)
                comb = (sig[:, hh:hh + 1] * o_c[hr] + sig[:, N_HEADS + hh:N_HEADS + hh + 1] * o_s[hr]
                        + sig[:, 2 * N_HEADS + hh:2 * N_HEADS + hh + 1] * o_w[hr])
                pieces.append(comb if g % 2 == sub else pltpu.roll(comb, HEAD_DIM, 1))
            oc_cols = slice((2 * g + pair) * LANES, (2 * g + pair + 1) * LANES)
            za = za_ref[:, oc_cols]
            o_ref[:, oc_cols] = (jnp.where(half == 0, pieces[0], pieces[1]) * (za * _sigmoid(za))).astype(BF16)


def _pattn(ub, ck, cv, mmat, kaug, ubb, gates, ua):
    const = lambda r, c: (lambda i: (r, c))
    once = pl.Buffered(1)
    resident = lambda c: pl.BlockSpec((SEQ, KV_W), const(0, c), pipeline_mode=once)
    small = lambda shape: pl.BlockSpec(shape, const(0, 0))
    return pl.pallas_call(
        _pattn_kernel,
        grid=(SEQ // Q_BLOCK,),
        in_specs=[pl.BlockSpec((Q_BLOCK, ATTN_W), lambda i: (i, 0)),
                  small((N_CMP, KV_W)), small((N_CMP, KV_W)), small((N_CMP, NSB_P)),
                  pl.BlockSpec((KV_HEADS, SEQ, KV_W), lambda i: (0, 0, 0), pipeline_mode=once),
                  resident(B_VS), resident(B_KW), resident(B_VW),
                  pl.BlockSpec((Q_BLOCK, LANES), lambda i: (i, 0)),
                  pl.BlockSpec((Q_BLOCK, ATTN_W), lambda i: (i, 3))],
        out_specs=pl.BlockSpec((Q_BLOCK, ATTN_W), lambda i: (i, 0)),
        out_shape=jax.ShapeDtypeStruct((SEQ, ATTN_W), BF16),
        scratch_shapes=[pltpu.VMEM((N_HEADS * Q_BLOCK, KV_W), BF16),
                        pltpu.VMEM((N_HEADS * Q_BLOCK, KV_W), BF16),
                        pltpu.VMEM((N_HEADS * Q_BLOCK, KV_W), F32),
                        pltpu.VMEM((N_HEADS * Q_BLOCK, KV_W), F32),
                        pltpu.VMEM((N_HEADS * Q_BLOCK, LANES), F32),
                        pltpu.VMEM((N_HEADS * Q_BLOCK, LANES), F32)],
        compiler_params=pltpu.CompilerParams(dimension_semantics=("parallel",), vmem_limit_bytes=VMEM_LIMIT),
        name="attn_prompt",
    )(ub, ck, cv, mmat, kaug, ubb, ubb, ubb, gates, ua)


SLC_PAGES = 16
SLC_CHUNKS = N_PAGES // SLC_PAGES
SLC_CHUNK_ROWS = SLC_PAGES * PAGE_SIZE
S_ROWS = N_HEADS * DEC_SEQ
NEW_PAD = LANES


def _sattn_kernel(pt_ref, qt_ref, ck_ref, cv_ref, mm_ref, ksc_ref, vsc_ref, ksn_ref, vsn_ref,
                  kwc_ref, vwc_ref, kwn_ref, vwn_ref, gt_ref, za_ref, o_ref,
                  kbuf, vbuf, ksem, vsem, newk_ref, newv_ref, acc_ref, m_ref, l_ref, *, layer):
    b = pl.program_id(0)
    nb = pl.num_programs(0)
    R = S_ROWS
    row = lax.broadcasted_iota(jnp.int32, (R, 1), 0)
    tcol = PAST_LEN + (row & (DEC_SEQ - 1))
    lane_group = lax.shift_right_arithmetic(lax.broadcasted_iota(jnp.int32, (R, KV_W), 1), 6)
    diag = lane_group == lax.shift_right_arithmetic(row, 5)

    def chunk_copies(bb, c, slot, go):
        def body(p, carry):
            for cache, buf, sem in ((ksc_ref, kbuf, ksem), (vsc_ref, vbuf, vsem)):
                cp = pltpu.make_async_copy(cache.at[layer, pt_ref[bb, c * SLC_PAGES + p]],
                                           buf.at[slot, :, pl.ds(pl.multiple_of(p * PAGE_SIZE, PAGE_SIZE), PAGE_SIZE)],
                                           sem.at[slot])
                cp.start() if go else cp.wait()
            return carry
        lax.fori_loop(0, SLC_PAGES, body, 0)

    @pl.when(b == 0)
    def _():
        chunk_copies(0, 0, 0, True)

    q = jnp.where(diag, qt_ref[0] * SCALE, 0.0).astype(BF16)

    o_c, p = _cmp_branch(q, tcol, ck_ref[0], cv_ref[0])
    T = DEC_SEQ
    imp_rows = _dot(p.astype(BF16), mm_ref[...])
    groups = []
    for r0 in range(0, R, GQA * T):
        grp = imp_rows[r0:r0 + T] + imp_rows[r0 + T:r0 + 2 * T] + imp_rows[r0 + 2 * T:r0 + 3 * T] + imp_rows[r0 + 3 * T:r0 + 4 * T]
        groups += [grp] * GQA
    imp = jnp.concatenate(groups, axis=0)
    trow = PAST_LEN + (lax.broadcasted_iota(jnp.int32, (1, R), 1) & (T - 1))
    a = _select_bias(imp.T, trow, NSB_S, axis=0).T.astype(BF16)

    m_ref[...] = jnp.full(m_ref.shape, NEG, F32)
    l_ref[...] = jnp.zeros(l_ref.shape, F32)
    acc_ref[...] = jnp.zeros(acc_ref.shape, F32)
    allrows = slice(0, R)
    for c in range(SLC_CHUNKS):
        slot = c % 2
        if c + 1 < SLC_CHUNKS:
            chunk_copies(b, c + 1, 1 - slot, True)
        else:
            @pl.when(b + 1 < nb)
            def _():
                chunk_copies(b + 1, 0, 1 - slot, True)
        chunk_copies(b, c, slot, False)
        onehot = _block_onehot(NSB_S_PAD, SLC_CHUNK_ROWS, c * SLC_CHUNK_ROWS // SEL_BLOCK)
        s = _dot(q, kbuf[slot].astype(BF16)) + _dot(a, onehot)
        _online_update(s, vbuf[slot].astype(BF16), m_ref, l_ref, acc_ref, allrows, v_transposed=True)

    newcol = lax.broadcasted_iota(jnp.int32, (1, NEW_PAD), 1)
    new_ok = PAST_LEN + newcol <= tcol
    newk_ref[...] = jnp.zeros(newk_ref.shape, BF16)
    newv_ref[...] = jnp.zeros(newv_ref.shape, BF16)
    newk_ref[0:T, :] = ksn_ref[0].astype(BF16)
    newv_ref[0:T, :] = vsn_ref[0].astype(BF16)
    onehot = _block_onehot(NSB_S_PAD, NEW_PAD, PAST_LEN // SEL_BLOCK)
    s = jnp.where(new_ok, _dot_nt(q, newk_ref[...]) + _dot(a, onehot), NEG)
    _online_update(s, newv_ref[...], m_ref, l_ref, acc_ref, allrows)
    o_s = acc_ref[...] / _lane_tile(l_ref[...], KV_W // LANES)

    wpos = (PAST_LEN - WINDOW) + lax.broadcasted_iota(jnp.int32, (1, WINDOW), 1)
    newk_ref[0:T, :] = kwn_ref[0].astype(BF16)
    newv_ref[0:T, :] = vwn_ref[0].astype(BF16)
    s3 = jnp.concatenate(
        [jnp.where(_window_mask(tcol, wpos), _dot(q, kwc_ref[0, 0].astype(BF16)), NEG),
         jnp.where(new_ok, _dot_nt(q, newk_ref[...]), NEG)], axis=1)
    e3 = jnp.exp2(s3 - jnp.max(s3, axis=1, keepdims=True))
    p3 = e3.astype(BF16)
    o_w = (_dot_nt(p3[:, 0:WINDOW], vwc_ref[0, 0].astype(BF16)) + _dot(p3[:, WINDOW:], newv_ref[...]))
    o_w = o_w / jnp.sum(e3, axis=1, keepdims=True)

    sig = _sigmoid(gt_ref[0])
    comb = jnp.where(diag, sig[:, 0:1] * o_c + sig[:, 1:2] * o_s + sig[:, 2:3] * o_w, 0.0)
    folded = comb[:, 0:LANES] + comb[:, LANES:2 * LANES]
    folded = folded + pltpu.roll(folded, HEAD_DIM, 1)
    za = za_ref[0]
    o_ref[0] = folded[:, 0:HEAD_DIM] * (za * _sigmoid(za))


def _sattn(page_table, qt, ck, cv, mmat, cache_k, cache_v, ksn, vsn, win_k, win_v, kwn, vwn, gates, za, layer):
    per_b = lambda shape: pl.BlockSpec((1,) + shape, lambda b, pt: (b,) + (0,) * len(shape))
    win = pl.BlockSpec((1, 1, KV_W, WINDOW), lambda b, pt: (layer, b, 0, 0))
    hbm = pl.BlockSpec(memory_space=pl.ANY)
    return pl.pallas_call(
        functools.partial(_sattn_kernel, layer=layer),
        grid_spec=pltpu.PrefetchScalarGridSpec(
            num_scalar_prefetch=1, grid=(DEC_BATCH,),
            in_specs=[per_b((S_ROWS, KV_W)), per_b((N_CMP, KV_W)), per_b((N_CMP, KV_W)),
                      pl.BlockSpec((N_CMP, NSB_S_PAD), lambda b, pt: (0, 0)),
                      hbm, hbm, per_b((DEC_SEQ, KV_W)), per_b((DEC_SEQ, KV_W)),
                      win, win, per_b((DEC_SEQ, KV_W)), per_b((DEC_SEQ, KV_W)),
                      per_b((S_ROWS, LANES)), per_b((S_ROWS, HEAD_DIM))],
            out_specs=per_b((S_ROWS, HEAD_DIM)),
            scratch_shapes=[pltpu.VMEM((2, KV_W, SLC_CHUNK_ROWS), F32), pltpu.VMEM((2, KV_W, SLC_CHUNK_ROWS), F32),
                            pltpu.SemaphoreType.DMA((2,)), pltpu.SemaphoreType.DMA((2,)),
                            pltpu.VMEM((NEW_PAD, KV_W), BF16), pltpu.VMEM((NEW_PAD, KV_W), BF16),
                            pltpu.VMEM((S_ROWS, KV_W), F32),
                            pltpu.VMEM((S_ROWS, LANES), F32), pltpu.VMEM((S_ROWS, LANES), F32)]),
        out_shape=jax.ShapeDtypeStruct((DEC_BATCH, S_ROWS, HEAD_DIM), F32),
        compiler_params=pltpu.CompilerParams(dimension_semantics=("arbitrary",), vmem_limit_bytes=VMEM_LIMIT),
        name="attn_sample",
    )(page_table, qt, ck, cv, mmat, cache_k, cache_v, ksn, vsn, win_k, win_v, kwn, vwn, gates, za)


OUT_TM = 256


def _outproj_kernel(h_ref, yc_ref, ya_ref, pe_ref, wc_ref, wa_ref, wg_ref, wp_ref, fg_ref, o_ref, *, final):
    h1 = h_ref[...] + (_dot(yc_ref[...], wc_ref[...]) + _dot(ya_ref[...], wa_ref[...]))
    h2 = h1 + _sigmoid(_dot(h1.astype(BF16), wg_ref[...])) * _dot(pe_ref[...].astype(BF16), wp_ref[...])
    if final:
        h2 = (h2 * lax.rsqrt(jnp.mean(h2 * h2, axis=-1, keepdims=True) + EPS)) * fg_ref[...]
    o_ref[...] = h2


def _outproj(h, yc, ya, pe, wc, wa, wg, wp, fg, final):
    row = lambda w: pl.BlockSpec((OUT_TM, w), lambda i: (i, 0))
    const = lambda shape: pl.BlockSpec(shape, lambda i: (0, 0), pipeline_mode=pl.Buffered(1))
    return pl.pallas_call(
        functools.partial(_outproj_kernel, final=final),
        grid=(N_ROWS // OUT_TM,),
        in_specs=[row(D_MODEL), row(C_CONV), row(ATTN_W), row(PLE_DIM),
                  const((C_CONV, D_MODEL)), const((ATTN_W, D_MODEL)), const((D_MODEL, D_MODEL)),
                  const((PLE_DIM, D_MODEL)), const((1, D_MODEL))],
        out_specs=row(D_MODEL),
        out_shape=jax.ShapeDtypeStruct((N_ROWS, D_MODEL), F32),
        compiler_params=pltpu.CompilerParams(dimension_semantics=("parallel",), vmem_limit_bytes=VMEM_LIMIT),
        name="outproj_final" if final else "outproj",
    )(h, yc, ya, pe, wc, wa, wg, wp, fg)


def _rope_tables():
    pos = jnp.concatenate([jnp.arange(SEQ), PAST_LEN + jnp.tile(jnp.arange(DEC_SEQ), DEC_BATCH)])
    half = ROT_DIM // 2
    inv = ROPE_THETA ** (-2.0 * jnp.arange(half, dtype=F32) / ROT_DIM)
    ang = pos.astype(F32)[:, None] * inv[None, :]
    cos, sin = jnp.cos(ang), jnp.sin(ang)
    one = jnp.ones((N_ROWS, HEAD_DIM - ROT_DIM), F32)
    zero = lambda n: jnp.zeros((N_ROWS, n), F32)
    c = jnp.concatenate([cos, cos, one], axis=1)
    sa = jnp.concatenate([-sin, zero(HEAD_DIM - half)], axis=1)
    sb = jnp.concatenate([zero(half), sin, zero(HEAD_DIM - ROT_DIM)], axis=1)
    return tuple(jnp.concatenate([t, t], axis=1) for t in (c, sa, sb))


def _cmp_to_sel(nsb, pad):
    n = jnp.arange(N_CMP)[:, None] * CMP_STRIDE
    j = jnp.arange(pad)[None, :]
    m = (n < (j + 1) * SEL_BLOCK) & (n + CMP_LEN > j * SEL_BLOCK) & (j < nsb)
    return m.astype(BF16)


def _expand_cmp_weight(w):
    w3 = w.reshape(CMP_LEN, HEAD_DIM, HEAD_DIM)
    eye = jnp.eye(KV_HEADS, dtype=w.dtype)
    return jnp.einsum('lde,gh->lgdhe', w3, eye).reshape(CMP_LEN, KV_W, KV_W).astype(BF16)


def _pages_transposed(cache):
    d, n, p = cache.shape[:3]
    return cache.transpose(0, 1, 3, 4, 2).reshape(d, n, KV_W, p)


def _to_head_rows(x, width):
    return x.reshape(DEC_BATCH, DEC_SEQ, N_HEADS, width).transpose(0, 2, 1, 3).reshape(DEC_BATCH, S_ROWS, width)


def kernel(x_prompt, x_sample, cache_cmp_k, cache_cmp_v, cache_slc_k, cache_slc_v, cache_win_k, cache_win_v,
           state_conv, page_table, p_prompt, p_sample, norm_g, w_in, conv_w, conv_b, conv_ln_g, conv_ln_b,
           w_cmp_k, w_cmp_v, w_out, w_ple, w_ple_gate, final_g):
    h = jnp.concatenate([x_prompt.reshape(SEQ, D_MODEL), x_sample.reshape(N_SAMPLE, D_MODEL)], axis=0)
    cos, sa, sb = _rope_tables()
    mm_p = _cmp_to_sel(NSB_P, NSB_P)
    mm_s = _cmp_to_sel(NSB_S, NSB_S_PAD)
    cck, ccv = _pages_transposed(cache_cmp_k), _pages_transposed(cache_cmp_v)
    csk, csv = _pages_transposed(cache_slc_k), _pages_transposed(cache_slc_v)
    cwk, cwv = _pages_transposed(cache_win_k), _pages_transposed(cache_win_v)
    page_row = jnp.arange(2 * PAGE_SIZE)
    in_page = page_row % PAGE_SIZE
    perm_col = (page_row - in_page) + (in_page % CMP_STRIDE) * PAGE_CHUNKS + in_page // CMP_STRIDE
    perm = (page_row[:, None] == perm_col[None, :]).astype(BF16)
    hist_p = jnp.zeros((1, HIST_PAD, C_CONV), F32)
    hist_s = jnp.pad(state_conv, ((0, 0), (0, 0), (HIST_OFF, 0), (0, 0)))

    o_q = 3 * C_CONV
    o_kv = o_q + ATTN_W
    o_gt = o_kv + 6 * KV_W
    o_za = o_gt + 3 * N_HEADS
    kv = lambda wi, n: wi[:, o_kv + n * KV_W:o_kv + (n + 1) * KV_W]

    st_p, st_s = [], []
    for i in range(DEPTH):
        wi = w_in[i]
        w_a = jnp.concatenate([wi[:, :o_q], wi[:, o_za:]], axis=1).astype(BF16)
        w_b = jnp.concatenate([wi[:, o_q:o_kv], kv(wi, 0), kv(wi, 2), kv(wi, 4), kv(wi, 1), kv(wi, 3), kv(wi, 5)],
                              axis=1).astype(BF16)
        w_g = jnp.pad(wi[:, o_gt:o_za], ((0, 0), (0, LANES - 3 * N_HEADS))).astype(BF16)
        g_row = norm_g[i].reshape(1, D_MODEL)

        ua, = _inproj(h, g_row, w_a)
        ub, ubb, gates = _inproj(h, g_row, w_b, extras=(w_g, cos, sa, sb))

        conv_args = (conv_w[i], conv_b[i].reshape(1, C_CONV), conv_ln_g[i].reshape(1, C_CONV),
                     conv_ln_b[i].reshape(1, C_CONV))
        yc_p, nc_p = _conv(ua, hist_p, *conv_args, row0=0, batch=1, t_len=SEQ, tm=256)
        yc_s, nc_s = _conv(ua, hist_s[i], *conv_args, row0=SEQ, batch=DEC_BATCH, t_len=DEC_SEQ, tm=DEC_SEQ)

        seg = lambda n: ub[:, ATTN_W + n * KV_W:ATTN_W + (n + 1) * KV_W]
        kc, ks, kw, vc, vs, vw = (seg(n) for n in range(6))
        wab_k, wab_v = _expand_cmp_weight(w_cmp_k[i]), _expand_cmp_weight(w_cmp_v[i])

        ck_p = _pcompress(ub, B_KC, wab_k)
        cv_p = _pcompress(ub, B_VC, wab_v)
        ya_p = _pattn(ub, ck_p, cv_p, mm_p, _pack_keys(ub, B_KS), ubb, gates, ua)

        new_rows = lambda x: x[SEQ:].reshape(DEC_BATCH, DEC_SEQ, KV_W)
        halves = lambda w: jnp.concatenate([w[:CMP_STRIDE].reshape(CHUNK_W, KV_W), w[CMP_STRIDE:].reshape(CHUNK_W, KV_W)],
                                           axis=1)
        ck_s = _scompress(page_table, cck, new_rows(kc), perm, halves(wab_k), i)
        cv_s = _scompress(page_table, ccv, new_rows(vc), perm, halves(wab_v), i)
        qt = jnp.tile(_to_head_rows(ub[SEQ:, :ATTN_W], HEAD_DIM), (1, 1, KV_HEADS))
        gt_s = gates[SEQ:, :3 * N_HEADS].reshape(DEC_BATCH, DEC_SEQ, 3, N_HEADS).transpose(0, 3, 1, 2)
        gt_s = jnp.pad(gt_s.reshape(DEC_BATCH, S_ROWS, 3), ((0, 0), (0, 0), (0, LANES - 3)))
        za_s = _to_head_rows(ua[SEQ:, 3 * C_CONV:], HEAD_DIM)
        ya_s = _sattn(page_table, qt, ck_s, cv_s, mm_s, csk, csv, new_rows(ks), new_rows(vs),
                      cwk, cwv, new_rows(kw), new_rows(vw), gt_s, za_s, i)
        ya_s = ya_s.reshape(DEC_BATCH, N_HEADS, DEC_SEQ, HEAD_DIM).transpose(0, 2, 1, 3).reshape(N_SAMPLE, ATTN_W)

        yc = jnp.concatenate([yc_p, yc_s.astype(BF16)], axis=0)
        ya = jnp.concatenate([ya_p, ya_s.astype(BF16)], axis=0)
        pe = jnp.concatenate([p_prompt[i].reshape(SEQ, PLE_DIM), p_sample[i].reshape(N_SAMPLE, PLE_DIM)], axis=0)
        wo = w_out[i].astype(BF16)
        h = _outproj(h, yc, ya, pe, wo[:C_CONV], wo[C_CONV:], w_ple_gate[i].astype(BF16), w_ple[i].astype(BF16),
                     final_g.reshape(1, D_MODEL), final=(i == DEPTH - 1))

        heads = lambda x, b, t: x.reshape(b, t, KV_HEADS, HEAD_DIM)
        wb = min(WINDOW, SEQ)
        st_p.append((heads(kc[:SEQ], 1, SEQ), heads(vc[:SEQ], 1, SEQ), heads(ks[:SEQ], 1, SEQ), heads(vs[:SEQ], 1, SEQ),
                     heads(kw[SEQ - wb:SEQ], 1, wb), heads(vw[SEQ - wb:SEQ], 1, wb), nc_p[:, HIST_OFF:]))
        hs = lambda x: heads(x[SEQ:], DEC_BATCH, DEC_SEQ)
        st_s.append((hs(kc), hs(vc), hs(ks), hs(vs),
                     jnp.concatenate([cache_win_k[i][:, DEC_SEQ:], hs(kw)], axis=1),
                     jnp.concatenate([cache_win_v[i][:, DEC_SEQ:], hs(vw)], axis=1), nc_s[:, HIST_OFF:]))

    y_prompt = h[:SEQ].reshape(1, SEQ, D_MODEL)
    y_sample = h[SEQ:].reshape(DEC_BATCH, DEC_SEQ, D_MODEL)
    outs_p = [jnp.stack([s[n] for s in st_p]) for n in range(7)]
    outs_s = [jnp.stack([s[n] for s in st_s]) for n in range(7)]
    return (y_prompt, y_sample, *outs_p, *outs_s)
```

```python
import functools

import jax
import jax.numpy as jnp
from jax import lax
from jax.experimental import pallas as pl
from jax.experimental.pallas import tpu as pltpu

F32 = jnp.float32
BF16 = jnp.bfloat16

D_MODEL = 2048
SEQ = 8192
DEPTH = 2
DEC_BATCH = 32
DEC_SEQ = 8
PAST_LEN = 8192
PAGE_SIZE = 128
HEAD_DIM = 64
ATTN_W = D_MODEL // 2
C_CONV = D_MODEL - ATTN_W
N_HEADS = ATTN_W // HEAD_DIM
KV_HEADS = N_HEADS // 4
GQA = N_HEADS // KV_HEADS
KV_W = KV_HEADS * HEAD_DIM
ROT_DIM = HEAD_DIM // 4
ROPE_THETA = 500000.0
CONV_WIDTH = 31
CMP_LEN = 32
CMP_STRIDE = 16
SEL_BLOCK = 64
SEL_TOPK = 16
WINDOW = 512
Q_BLOCK = 128
PLE_DIM = 256
EPS = 1e-6
NEG = -1e30
BIG = 1e9

N_SAMPLE = DEC_BATCH * DEC_SEQ
N_PAGES = PAST_LEN // PAGE_SIZE
N_CHUNK = PAST_LEN // CMP_STRIDE
N_CMP = N_CHUNK
PAGE_CHUNKS = PAGE_SIZE // CMP_STRIDE
NSB_P = SEQ // SEL_BLOCK
NSB_S = -(-(PAST_LEN + DEC_SEQ) // SEL_BLOCK)
NSB_S_PAD = 256
BELOW_NEG = -3e38
LOG2E = 1.4426950408889634
SCALE = HEAD_DIM ** -0.5 * LOG2E

LANES = 128
SUBLANES = 8
VMEM_LIMIT = 56 * 1024 * 1024

A_COLS = 4 * C_CONV
B_COLS = ATTN_W + 6 * KV_W
B_ROPE_COLS = ATTN_W + 3 * KV_W
B_KC, B_KS, B_KW, B_VC, B_VS, B_VW = (ATTN_W // KV_W + n for n in range(6))
PROJ_TM = 1024
PROJ_TN = 512
SLC_KT = 1024


def _dot(a, b):
    return jnp.dot(a, b, preferred_element_type=F32)


def _dot_nt(a, b):
    return lax.dot_general(a, b, (((1,), (1,)), ((), ())), preferred_element_type=F32)


def _lane_tile(x, n):
    return x if n == 1 else jnp.concatenate([x] * n, axis=1)


def _sigmoid(x):
    return 1.0 / (1.0 + jnp.exp(-x))


def _inproj_kernel(*refs, rope_chunks, extras):
    if extras:
        x_ref, g_ref, w_ref, wg_ref, cos_ref, sa_ref, sb_ref, u_ref, ub_ref, gate_ref, hn_ref = refs
    else:
        x_ref, g_ref, w_ref, u_ref, hn_ref = refs
    j = pl.program_id(1)

    @pl.when(j == 0)
    def _():
        x = x_ref[...]
        r = lax.rsqrt(jnp.mean(x * x, axis=-1, keepdims=True) + EPS)
        hn = ((x * r) * g_ref[...]).astype(BF16)
        hn_ref[...] = hn
        if extras:
            gate_ref[...] = _dot(hn, wg_ref[...])

    acc = _dot(hn_ref[...], w_ref[...])
    if not extras:
        u_ref[...] = acc
        return

    def store(n_rope):
        c, sa, sb = cos_ref[...], sa_ref[...], sb_ref[...]
        for k in range(acc.shape[1] // LANES):
            cs = slice(k * LANES, (k + 1) * LANES)
            y = acc[:, cs]
            if k < n_rope:
                y = y * c + pltpu.roll(y, LANES - ROT_DIM // 2, 1) * sa + pltpu.roll(y, ROT_DIM // 2, 1) * sb
            u_ref[:, cs] = y
            ub_ref[:, cs] = y.astype(BF16)

    full_tiles, part_chunks = rope_chunks // (acc.shape[1] // LANES), rope_chunks % (acc.shape[1] // LANES)
    pl.when(j < full_tiles)(lambda: store(acc.shape[1] // LANES))
    if part_chunks:
        pl.when(j == full_tiles)(lambda: store(part_chunks))
    pl.when(j >= full_tiles + bool(part_chunks))(lambda: store(0))


def _inproj(x, g, w, extras=None):
    n_rows, n_cols = x.shape[0], w.shape[1]
    tm, tn = min(PROJ_TM, n_rows), PROJ_TN
    grid = (n_rows // tm, n_cols // tn)
    row = lambda i, j: (i, 0)
    in_specs = [pl.BlockSpec((tm, D_MODEL), row),
                pl.BlockSpec((1, D_MODEL), lambda i, j: (0, 0)),
                pl.BlockSpec((D_MODEL, tn), lambda i, j: (0, j))]
    out_specs = [pl.BlockSpec((tm, tn), lambda i, j: (i, j))]
    out_shape = [jax.ShapeDtypeStruct((n_rows, n_cols), F32)]
    args = [x, g, w]
    if extras:
        wg, cos, sa, sb = extras
        in_specs += [pl.BlockSpec((D_MODEL, LANES), lambda i, j: (0, 0))] + [pl.BlockSpec((tm, LANES), row)] * 3
        out_specs += [pl.BlockSpec((tm, tn), lambda i, j: (i, j)), pl.BlockSpec((tm, LANES), row)]
        out_shape += [jax.ShapeDtypeStruct((n_rows, n_cols), BF16), jax.ShapeDtypeStruct((n_rows, LANES), F32)]
        args += [wg, cos, sa, sb]
    return pl.pallas_call(
        functools.partial(_inproj_kernel, rope_chunks=B_ROPE_COLS // LANES, extras=bool(extras)),
        grid=grid, in_specs=in_specs, out_specs=out_specs, out_shape=out_shape,
        scratch_shapes=[pltpu.VMEM((tm, D_MODEL), BF16)],
        compiler_params=pltpu.CompilerParams(dimension_semantics=("parallel", "arbitrary"),
                                             vmem_limit_bytes=VMEM_LIMIT),
        name="inproj_b" if extras else "inproj_a",
    )(*args)


HIST_PAD = 32
HIST_OFF = HIST_PAD - (CONV_WIDTH - 1)


def _conv_kernel(a_ref, b_ref, zc_ref, hist_ref, w_ref, cb_ref, lg_ref, lb_ref, yc_ref, nc_ref, ext_ref, y_ref, *, tm):
    t = pl.program_id(1)

    @pl.when(t == 0)
    def _():
        ext_ref[0:HIST_PAD, :] = hist_ref[0]

    ext_ref[HIST_PAD:HIST_PAD + tm, :] = a_ref[...] * _sigmoid(b_ref[...])
    for c in range(C_CONV // LANES):
        cs = slice(c * LANES, (c + 1) * LANES)
        acc = jnp.zeros((tm, LANES), F32) + cb_ref[:, cs]
        for k in range(CONV_WIDTH):
            acc = acc + ext_ref[HIST_OFF + k:HIST_OFF + k + tm, cs] * w_ref[k:k + 1, cs]
        y_ref[:, cs] = acc
    y = y_ref[...]
    mu = jnp.mean(y, axis=-1, keepdims=True)
    yc = y - mu
    yn = yc * lax.rsqrt(jnp.mean(yc * yc, axis=-1, keepdims=True) + EPS) * lg_ref[...] + lb_ref[...]
    zc = zc_ref[...]
    yc_ref[...] = ((yn * _sigmoid(yn)) * (zc * _sigmoid(zc))).astype(yc_ref.dtype)
    tail = ext_ref[tm:tm + HIST_PAD, :]
    ext_ref[0:HIST_PAD, :] = tail

    @pl.when(t == pl.num_programs(1) - 1)
    def _():
        nc_ref[0] = tail


def _conv(ua, hist, w, cb, lg, lb, *, batch, t_len, tm):
    nt = t_len // tm
    col = lambda c: (lambda b, t: (b * nt + t, c))
    vec = pl.BlockSpec((1, C_CONV), lambda b, t: (0, 0))
    return pl.pallas_call(
        functools.partial(_conv_kernel, tm=tm),
        grid=(batch, nt),
        in_specs=[pl.BlockSpec((tm, C_CONV), col(0)), pl.BlockSpec((tm, C_CONV), col(1)),
                  pl.BlockSpec((tm, C_CONV), col(2)),
                  pl.BlockSpec((1, HIST_PAD, C_CONV), lambda b, t: (b, 0, 0)),
                  pl.BlockSpec((CONV_WIDTH, C_CONV), lambda b, t: (0, 0)), vec, vec, vec],
        out_specs=[pl.BlockSpec((tm, C_CONV), lambda b, t: (b * nt + t, 0)),
                   pl.BlockSpec((1, HIST_PAD, C_CONV), lambda b, t: (b, 0, 0))],
        out_shape=[jax.ShapeDtypeStruct((batch * t_len, C_CONV), BF16 if tm % 16 == 0 else F32),
                   jax.ShapeDtypeStruct((batch, HIST_PAD, C_CONV), F32)],
        scratch_shapes=[pltpu.VMEM((HIST_PAD + tm, C_CONV), F32), pltpu.VMEM((tm, C_CONV), F32)],
        compiler_params=pltpu.CompilerParams(dimension_semantics=("parallel", "arbitrary"),
                                             vmem_limit_bytes=VMEM_LIMIT),
        name="conv_group",
    )(ua, ua, ua, hist, w, cb, lg, lb)


CMP_ROWS = PAST_LEN + CMP_STRIDE


LANE_HALVES = KV_W // LANES


def _compress_rows(x_ref, w_ref):
    acc = None
    for l in range(CMP_LEN):
        xl = jnp.concatenate([x_ref[c, pl.ds(l, N_CMP, stride=CMP_STRIDE), :] for c in range(LANE_HALVES)], axis=1)
        part = _dot(xl.astype(BF16), w_ref[l])
        acc = part if acc is None else acc + part
    return acc


def _pcompress_kernel(x_ref, w_ref, o_ref, rows_ref):
    for c in range(LANE_HALVES):
        rows_ref[c, 0:SEQ, :] = x_ref[:, c * LANES:(c + 1) * LANES]
        rows_ref[c, SEQ:, :] = jnp.zeros((CMP_ROWS - SEQ, LANES), F32)
    o_ref[...] = _compress_rows(rows_ref, w_ref).astype(o_ref.dtype)


def _pcompress(ub, col_block, w):
    once = pl.Buffered(1)
    return pl.pallas_call(
        _pcompress_kernel,
        grid=(1,),
        in_specs=[pl.BlockSpec((SEQ, KV_W), lambda i: (0, col_block), pipeline_mode=once),
                  pl.BlockSpec((CMP_LEN, KV_W, KV_W), lambda i: (0, 0, 0), pipeline_mode=once)],
        out_specs=pl.BlockSpec((N_CMP, KV_W), lambda i: (0, 0)),
        out_shape=jax.ShapeDtypeStruct((N_CMP, KV_W), BF16),
        scratch_shapes=[pltpu.VMEM((LANE_HALVES, CMP_ROWS, LANES), F32)],
        compiler_params=pltpu.CompilerParams(dimension_semantics=("arbitrary",), vmem_limit_bytes=VMEM_LIMIT),
        name="compress_prompt",
    )(ub, w)


S_CHUNKS = N_CHUNK + SUBLANES


def _scompress_kernel(pt_ref, cache_ref, new_ref, perm_ref, wab_ref, o_ref, tbuf, xc_ref, sem, *, layer):
    b = pl.program_id(0)
    slot = b % 2

    def copies(bb, sl, go):
        def body(p, carry):
            cp = pltpu.make_async_copy(cache_ref.at[layer, pt_ref[bb, p]],
                                       tbuf.at[sl, :, pl.ds(pl.multiple_of(p * PAGE_SIZE, PAGE_SIZE), PAGE_SIZE)],
                                       sem.at[sl])
            cp.start() if go else cp.wait()
            return carry
        lax.fori_loop(0, N_PAGES, body, 0)

    @pl.when(b == 0)
    def _():
        copies(0, 0, True)

    @pl.when(b + 1 < pl.num_programs(0))
    def _():
        copies(b + 1, 1 - slot, True)

    copies(b, slot, False)
    perm = perm_ref[...]
    for pp in range(N_PAGES // 2):
        pages = tbuf[slot, :, pp * 2 * PAGE_SIZE:(pp + 1) * 2 * PAGE_SIZE].T.astype(BF16)
        rows = _dot(perm, pages)
        for k in range(2):
            p = 2 * pp + k
            for l in range(CMP_STRIDE):
                r0 = k * PAGE_SIZE + l * PAGE_CHUNKS
                for c in range(LANE_HALVES):
                    xc_ref[c, p * PAGE_CHUNKS:(p + 1) * PAGE_CHUNKS, l * LANES:(l + 1) * LANES] = (
                        rows[r0:r0 + PAGE_CHUNKS, c * LANES:(c + 1) * LANES])
    for c in range(LANE_HALVES):
        cs = slice(c * LANES, (c + 1) * LANES)
        xc_ref[c, N_CHUNK:, :] = jnp.zeros((SUBLANES, CMP_STRIDE * LANES), F32)
        for l in range(DEC_SEQ):
            xc_ref[c, N_CHUNK:N_CHUNK + 1, l * LANES:(l + 1) * LANES] = new_ref[0, l:l + 1, cs]
        z = _dot(xc_ref[c].astype(BF16), wab_ref[c])
        zb = pltpu.roll(z[:, LANES:], S_CHUNKS - 1, 0)
        o_ref[0, :, cs] = (z[0:N_CMP, :LANES] + zb[0:N_CMP]).astype(o_ref.dtype)


def _scompress(page_table, cache_t, new_rows, perm, wab, layer):
    once = pl.Buffered(1)
    return pl.pallas_call(
        functools.partial(_scompress_kernel, layer=layer),
        grid_spec=pltpu.PrefetchScalarGridSpec(
            num_scalar_prefetch=1, grid=(DEC_BATCH,),
            in_specs=[pl.BlockSpec(memory_space=pl.ANY),
                      pl.BlockSpec((1, DEC_SEQ, KV_W), lambda b, pt: (b, 0, 0)),
                      pl.BlockSpec((2 * PAGE_SIZE, 2 * PAGE_SIZE), lambda b, pt: (0, 0), pipeline_mode=once),
                      pl.BlockSpec((LANE_HALVES, CMP_STRIDE * LANES, 2 * LANES), lambda b, pt: (0, 0, 0),
                                   pipeline_mode=once)],
            out_specs=pl.BlockSpec((1, N_CMP, KV_W), lambda b, pt: (b, 0, 0)),
            scratch_shapes=[pltpu.VMEM((2, KV_W, PAST_LEN), F32),
                            pltpu.VMEM((LANE_HALVES, S_CHUNKS, CMP_STRIDE * LANES), F32),
                            pltpu.SemaphoreType.DMA((2,))]),
        out_shape=jax.ShapeDtypeStruct((DEC_BATCH, N_CMP, KV_W), BF16),
        compiler_params=pltpu.CompilerParams(dimension_semantics=("arbitrary",), vmem_limit_bytes=VMEM_LIMIT),
        name="compress_sample",
    )(page_table, cache_t, new_rows, perm, wab)


def _cmp_branch(q, tcol, ck, cv):
    s = _dot_nt(q, ck)
    n = lax.broadcasted_iota(jnp.int32, s.shape, 1)
    vis = (n * CMP_STRIDE + (CMP_LEN - 1)) <= tcol
    sm = jnp.where(vis, s, NEG)
    e = jnp.where(vis, jnp.exp2(sm - jnp.max(sm, axis=1, keepdims=True)), 0.0)
    den = jnp.sum(e, axis=1, keepdims=True)
    p = e / jnp.where(den > 0.0, den, 1.0)
    return _dot(p.astype(BF16), cv), p


def _select_bias(imp, t, nsb, axis):
    j = lax.broadcasted_iota(jnp.int32, imp.shape, axis)
    cur = lax.shift_right_arithmetic(t, 6)
    forced = (j == 0) | (j == cur) | (j == cur - 1)
    score = jnp.where(j * SEL_BLOCK <= t, jnp.where(forced, BIG, imp), NEG)
    score = jnp.where(j < nsb, score, BELOW_NEG)
    jf = j.astype(F32)
    bias = jnp.full(imp.shape, NEG, F32)
    for _ in range(SEL_TOPK):
        mx = jnp.max(score, axis=axis, keepdims=True)
        idx = jnp.min(jnp.where(score == mx, jf, 1e9), axis=axis, keepdims=True)
        pick = jf == idx
        bias = jnp.where(pick, 0.0, bias)
        score = jnp.where(pick, BELOW_NEG, score)
    return bias


def _block_onehot(nb, kt, first_block):
    j = lax.broadcasted_iota(jnp.int32, (nb, kt), 0)
    col = lax.broadcasted_iota(jnp.int32, (nb, kt), 1)
    return jnp.where(j - lax.shift_right_arithmetic(col, 6) == first_block, 1.0, 0.0).astype(BF16)


def _online_update(s, v, m_ref, l_ref, acc_ref, rows, v_transposed=False):
    m_prev, l_prev = m_ref[rows, :], l_ref[rows, :]
    m_next = jnp.maximum(m_prev, jnp.max(s, axis=1, keepdims=True))
    alpha = jnp.exp2(m_prev - m_next)
    p = jnp.exp2(s - _lane_tile(m_next, s.shape[1] // LANES))
    l_ref[rows, :] = alpha * l_prev + jnp.sum(p, axis=1, keepdims=True)
    m_ref[rows, :] = m_next
    pv = _dot_nt(p.astype(BF16), v) if v_transposed else _dot(p.astype(BF16), v)
    acc_ref[rows, :] = acc_ref[rows, :] * _lane_tile(alpha, KV_W // LANES) + pv


def _window_mask(tcol, kpos):
    return lax.bitcast_convert_type(tcol - kpos, jnp.uint32) <= jnp.uint32(WINDOW)


GROUP_ROWS = GQA * Q_BLOCK


PACK_TM = 512


def _pack_keys_kernel(k_ref, o_ref):
    i = pl.program_id(0)
    lane = lax.broadcasted_iota(jnp.int32, (PACK_TM, LANES), 1)
    blk = lax.shift_right_arithmetic(i * PACK_TM + lax.broadcasted_iota(jnp.int32, (PACK_TM, LANES), 0), 6)
    lo = lane < HEAD_DIM
    hot_lo = jnp.where(lane - HEAD_DIM == blk, 1.0, 0.0)
    hot_hi = jnp.where(lane + HEAD_DIM == blk, 1.0, 0.0)
    for g in range(KV_HEADS):
        pair = k_ref[:, (g // 2) * LANES:(g // 2 + 1) * LANES]
        if g % 2:
            pair = pltpu.roll(pair, HEAD_DIM, 1)
        o_ref[g, :, 0:LANES] = jnp.where(lo, pair, hot_lo).astype(BF16)
        o_ref[g, :, LANES:2 * LANES] = jnp.where(lo, hot_hi, 0.0).astype(BF16)


def _pack_keys(ub, col_block):
    assert NSB_P == 2 * HEAD_DIM and KV_W == 2 * LANES
    return pl.pallas_call(
        _pack_keys_kernel,
        grid=(SEQ // PACK_TM,),
        in_specs=[pl.BlockSpec((PACK_TM, KV_W), lambda i: (i, col_block))],
        out_specs=pl.BlockSpec((KV_HEADS, PACK_TM, KV_W), lambda i: (0, i, 0)),
        out_shape=jax.ShapeDtypeStruct((KV_HEADS, SEQ, KV_W), BF16),
        compiler_params=pltpu.CompilerParams(dimension_semantics=("parallel",), vmem_limit_bytes=VMEM_LIMIT),
        name="pack_keys",
    )(ub)


def _pattn_kernel(q_ref, ck_ref, cv_ref, mm_ref, ka_ref, vs_ref, kw_ref, vw_ref, gt_ref, za_ref, o_ref,
                  qbd_ref, qa_ref, oc_ref, acc_ref, m_ref, l_ref):
    i = pl.program_id(0)
    q0 = i * Q_BLOCK
    T, R = Q_BLOCK, GROUP_ROWS
    lane = lax.broadcasted_iota(jnp.int32, (T, LANES), 1)
    half = lax.shift_right_arithmetic(lane, 6)
    lo = lane < HEAD_DIM
    tcol = q0 + (lax.broadcasted_iota(jnp.int32, (R, 1), 0) & (T - 1))
    trow = q0 + lax.broadcasted_iota(jnp.int32, (1, T), 1)

    zeros = jnp.zeros((T, LANES), BF16)
    for hh in range(N_HEADS):
        g = hh // GQA
        pair = q_ref[:, (hh // 2) * LANES:(hh // 2 + 1) * LANES] * SCALE
        if hh % 2 != g % 2:
            pair = pltpu.roll(pair, HEAD_DIM, 1)
        x = jnp.where(half == g % 2, pair, 0.0).astype(BF16)
        qbd_ref[hh * T:(hh + 1) * T, 0:LANES] = x if g // 2 == 0 else zeros
        qbd_ref[hh * T:(hh + 1) * T, LANES:2 * LANES] = zeros if g // 2 == 0 else x

    for g in range(KV_HEADS):
        rows = slice(g * R, (g + 1) * R)
        o_c, p = _cmp_branch(qbd_ref[rows, :], tcol, ck_ref[...], cv_ref[...])
        oc_ref[rows, :] = o_c
        imp_rows = _dot(p.astype(BF16), mm_ref[...])
        imp = imp_rows[0:T] + imp_rows[T:2 * T] + imp_rows[2 * T:3 * T] + imp_rows[3 * T:4 * T]
        bias = _select_bias(imp.T, trow, NSB_P, axis=0).T
        bias = pltpu.roll(bias, HEAD_DIM, 1)
        for h in range(GQA):
            hh = g * GQA + h
            pair = q_ref[:, (hh // 2) * LANES:(hh // 2 + 1) * LANES] * SCALE
            if hh % 2:
                pair = pltpu.roll(pair, HEAD_DIM, 1)
            qa_ref[hh * T:(hh + 1) * T, 0:LANES] = jnp.where(lo, pair, bias).astype(BF16)
            qa_ref[hh * T:(hh + 1) * T, LANES:2 * LANES] = jnp.where(lo, bias, 0.0).astype(BF16)

    m_ref[...] = jnp.full(m_ref.shape, NEG, F32)
    l_ref[...] = jnp.zeros(l_ref.shape, F32)
    acc_ref[...] = jnp.zeros(acc_ref.shape, F32)
    kcol = lax.broadcasted_iota(jnp.int32, (1, SLC_KT), 1)

    def slc_tile(c, causal):
        k0 = pl.multiple_of(c * SLC_KT, SLC_KT)
        v_t = vs_ref[pl.ds(k0, SLC_KT), :]
        for g in range(KV_HEADS):
            rows = slice(g * R, (g + 1) * R)
            s = _dot_nt(qa_ref[rows, :], ka_ref[g, pl.ds(k0, SLC_KT), :])
            if causal:
                s = jnp.where(k0 + kcol <= tcol, s, NEG)
            _online_update(s, v_t, m_ref, l_ref, acc_ref, rows)

    c_last = (q0 + T - 1) // SLC_KT

    def full_tile(c, carry):
        slc_tile(c, False)
        return carry

    lax.fori_loop(0, c_last, full_tile, 0)
    slc_tile(c_last, True)

    start = pl.multiple_of(jnp.maximum(q0 - WINDOW, 0), Q_BLOCK)
    kw_t = kw_ref[pl.ds(start, WINDOW + T), :]
    vw_t = vw_ref[pl.ds(start, WINDOW + T), :]
    kpos = start + lax.broadcasted_iota(jnp.int32, (1, WINDOW + T), 1)
    sig = _sigmoid(gt_ref[...])
    for g in range(KV_HEADS):
        rows = slice(g * R, (g + 1) * R)
        cc = slice((g // 2) * LANES, (g // 2 + 1) * LANES)
        s3 = jnp.where(_window_mask(tcol, kpos), _dot_nt(qbd_ref[rows, :], kw_t), NEG)
        e3 = jnp.exp2(s3 - jnp.max(s3, axis=1, keepdims=True))
        o_w = _dot(e3.astype(BF16), vw_t)[:, cc] / jnp.sum(e3, axis=1, keepdims=True)
        o_s = acc_ref[rows, cc] / l_ref[rows, :]
        o_c = oc_ref[rows, cc]
        for pair in range(2):
            pieces = []
            for sub in range(2):
                h = 2 * pair + sub
                hh = g * GQA + h
                hr = slice(h * T, (h + 1) * T)
                comb = (sig[:, hh:hh + 1] * o_c[hr] + sig[:, N_HEADS + hh:N_HEADS + hh + 1] * o_s[hr]
                        + sig[:, 2 * N_HEADS + hh:2 * N_HEADS + hh + 1] * o_w[hr])
                pieces.append(comb if g % 2 == sub else pltpu.roll(comb, HEAD_DIM, 1))
            oc_cols = slice((2 * g + pair) * LANES, (2 * g + pair + 1) * LANES)
            za = za_ref[:, oc_cols]
            o_ref[:, oc_cols] = (jnp.where(half == 0, pieces[0], pieces[1]) * (za * _sigmoid(za))).astype(BF16)


def _pattn(ub, ck, cv, mmat, kaug, ubb, gates, ua):
    const = lambda r, c: (lambda i: (r, c))
    once = pl.Buffered(1)
    resident = lambda c: pl.BlockSpec((SEQ, KV_W), const(0, c), pipeline_mode=once)
    small = lambda shape: pl.BlockSpec(shape, const(0, 0))
    return pl.pallas_call(
        _pattn_kernel,
        grid=(SEQ // Q_BLOCK,),
        in_specs=[pl.BlockSpec((Q_BLOCK, ATTN_W), lambda i: (i, 0)),
                  small((N_CMP, KV_W)), small((N_CMP, KV_W)), small((N_CMP, NSB_P)),
                  pl.BlockSpec((KV_HEADS, SEQ, KV_W), lambda i: (0, 0, 0), pipeline_mode=once),
                  resident(B_VS), resident(B_KW), resident(B_VW),
                  pl.BlockSpec((Q_BLOCK, LANES), lambda i: (i, 0)),
                  pl.BlockSpec((Q_BLOCK, ATTN_W), lambda i: (i, 3))],
        out_specs=pl.BlockSpec((Q_BLOCK, ATTN_W), lambda i: (i, 0)),
        out_shape=jax.ShapeDtypeStruct((SEQ, ATTN_W), BF16),
        scratch_shapes=[pltpu.VMEM((N_HEADS * Q_BLOCK, KV_W), BF16),
                        pltpu.VMEM((N_HEADS * Q_BLOCK, KV_W), BF16),
                        pltpu.VMEM((N_HEADS * Q_BLOCK, KV_W), F32),
                        pltpu.VMEM((N_HEADS * Q_BLOCK, KV_W), F32),
                        pltpu.VMEM((N_HEADS * Q_BLOCK, LANES), F32),
                        pltpu.VMEM((N_HEADS * Q_BLOCK, LANES), F32)],
        compiler_params=pltpu.CompilerParams(dimension_semantics=("parallel",), vmem_limit_bytes=VMEM_LIMIT),
        name="attn_prompt",
    )(ub, ck, cv, mmat, kaug, ubb, ubb, ubb, gates, ua)


SLC_PAGES = 16
SLC_CHUNKS = N_PAGES // SLC_PAGES
SLC_CHUNK_ROWS = SLC_PAGES * PAGE_SIZE
S_ROWS = N_HEADS * DEC_SEQ
NEW_PAD = LANES


def _sattn_kernel(pt_ref, qt_ref, ck_ref, cv_ref, mm_ref, ksc_ref, vsc_ref, ksn_ref, vsn_ref,
                  kwc_ref, vwc_ref, kwn_ref, vwn_ref, gt_ref, za_ref, o_ref,
                  kbuf, vbuf, ksem, vsem, newk_ref, newv_ref, acc_ref, m_ref, l_ref, *, layer):
    b = pl.program_id(0)
    nb = pl.num_programs(0)
    R = S_ROWS
    row = lax.broadcasted_iota(jnp.int32, (R, 1), 0)
    tcol = PAST_LEN + (row & (DEC_SEQ - 1))
    lane_group = lax.shift_right_arithmetic(lax.broadcasted_iota(jnp.int32, (R, KV_W), 1), 6)
    diag = lane_group == lax.shift_right_arithmetic(row, 5)

    def chunk_copies(bb, c, slot, go):
        def body(p, carry):
            for cache, buf, sem in ((ksc_ref, kbuf, ksem), (vsc_ref, vbuf, vsem)):
                cp = pltpu.make_async_copy(cache.at[layer, pt_ref[bb, c * SLC_PAGES + p]],
                                           buf.at[slot, :, pl.ds(pl.multiple_of(p * PAGE_SIZE, PAGE_SIZE), PAGE_SIZE)],
                                           sem.at[slot])
                cp.start() if go else cp.wait()
            return carry
        lax.fori_loop(0, SLC_PAGES, body, 0)

    @pl.when(b == 0)
    def _():
        chunk_copies(0, 0, 0, True)

    q = jnp.where(diag, qt_ref[0] * SCALE, 0.0).astype(BF16)

    o_c, p = _cmp_branch(q, tcol, ck_ref[0], cv_ref[0])
    T = DEC_SEQ
    imp_rows = _dot(p.astype(BF16), mm_ref[...])
    groups = []
    for r0 in range(0, R, GQA * T):
        grp = imp_rows[r0:r0 + T] + imp_rows[r0 + T:r0 + 2 * T] + imp_rows[r0 + 2 * T:r0 + 3 * T] + imp_rows[r0 + 3 * T:r0 + 4 * T]
        groups += [grp] * GQA
    imp = jnp.concatenate(groups, axis=0)
    trow = PAST_LEN + (lax.broadcasted_iota(jnp.int32, (1, R), 1) & (T - 1))
    a = _select_bias(imp.T, trow, NSB_S, axis=0).T.astype(BF16)

    m_ref[...] = jnp.full(m_ref.shape, NEG, F32)
    l_ref[...] = jnp.zeros(l_ref.shape, F32)
    acc_ref[...] = jnp.zeros(acc_ref.shape, F32)
    allrows = slice(0, R)
    for c in range(SLC_CHUNKS):
        slot = c % 2
        if c + 1 < SLC_CHUNKS:
            chunk_copies(b, c + 1, 1 - slot, True)
        else:
            @pl.when(b + 1 < nb)
            def _():
                chunk_copies(b + 1, 0, 1 - slot, True)
        chunk_copies(b, c, slot, False)
        onehot = _block_onehot(NSB_S_PAD, SLC_CHUNK_ROWS, c * SLC_CHUNK_ROWS // SEL_BLOCK)
        s = _dot(q, kbuf[slot].astype(BF16)) + _dot(a, onehot)
        _online_update(s, vbuf[slot].astype(BF16), m_ref, l_ref, acc_ref, allrows, v_transposed=True)

    newcol = lax.broadcasted_iota(jnp.int32, (1, NEW_PAD), 1)
    new_ok = PAST_LEN + newcol <= tcol
    newk_ref[...] = jnp.zeros(newk_ref.shape, BF16)
    newv_ref[...] = jnp.zeros(newv_ref.shape, BF16)
    newk_ref[0:T, :] = ksn_ref[0].astype(BF16)
    newv_ref[0:T, :] = vsn_ref[0].astype(BF16)
    onehot = _block_onehot(NSB_S_PAD, NEW_PAD, PAST_LEN // SEL_BLOCK)
    s = jnp.where(new_ok, _dot_nt(q, newk_ref[...]) + _dot(a, onehot), NEG)
    _online_update(s, newv_ref[...], m_ref, l_ref, acc_ref, allrows)
    o_s = acc_ref[...] / _lane_tile(l_ref[...], KV_W // LANES)

    wpos = (PAST_LEN - WINDOW) + lax.broadcasted_iota(jnp.int32, (1, WINDOW), 1)
    newk_ref[0:T, :] = kwn_ref[0].astype(BF16)
    newv_ref[0:T, :] = vwn_ref[0].astype(BF16)
    s3 = jnp.concatenate(
        [jnp.where(_window_mask(tcol, wpos), _dot(q, kwc_ref[0, 0].astype(BF16)), NEG),
         jnp.where(new_ok, _dot_nt(q, newk_ref[...]), NEG)], axis=1)
    e3 = jnp.exp2(s3 - jnp.max(s3, axis=1, keepdims=True))
    p3 = e3.astype(BF16)
    o_w = (_dot_nt(p3[:, 0:WINDOW], vwc_ref[0, 0].astype(BF16)) + _dot(p3[:, WINDOW:], newv_ref[...]))
    o_w = o_w / jnp.sum(e3, axis=1, keepdims=True)

    sig = _sigmoid(gt_ref[0])
    comb = jnp.where(diag, sig[:, 0:1] * o_c + sig[:, 1:2] * o_s + sig[:, 2:3] * o_w, 0.0)
    folded = comb[:, 0:LANES] + comb[:, LANES:2 * LANES]
    folded = folded + pltpu.roll(folded, HEAD_DIM, 1)
    za = za_ref[0]
    o_ref[0] = folded[:, 0:HEAD_DIM] * (za * _sigmoid(za))


def _sattn(page_table, qt, ck, cv, mmat, cache_k, cache_v, ksn, vsn, win_k, win_v, kwn, vwn, gates, za, layer):
    per_b = lambda shape: pl.BlockSpec((1,) + shape, lambda b, pt: (b,) + (0,) * len(shape))
    win = pl.BlockSpec((1, 1, KV_W, WINDOW), lambda b, pt: (layer, b, 0, 0))
    hbm = pl.BlockSpec(memory_space=pl.ANY)
    return pl.pallas_call(
        functools.partial(_sattn_kernel, layer=layer),
        grid_spec=pltpu.PrefetchScalarGridSpec(
            num_scalar_prefetch=1, grid=(DEC_BATCH,),
            in_specs=[per_b((S_ROWS, KV_W)), per_b((N_CMP, KV_W)), per_b((N_CMP, KV_W)),
                      pl.BlockSpec((N_CMP, NSB_S_PAD), lambda b, pt: (0, 0)),
                      hbm, hbm, per_b((DEC_SEQ, KV_W)), per_b((DEC_SEQ, KV_W)),
                      win, win, per_b((DEC_SEQ, KV_W)), per_b((DEC_SEQ, KV_W)),
                      per_b((S_ROWS, LANES)), per_b((S_ROWS, HEAD_DIM))],
            out_specs=per_b((S_ROWS, HEAD_DIM)),
            scratch_shapes=[pltpu.VMEM((2, KV_W, SLC_CHUNK_ROWS), F32), pltpu.VMEM((2, KV_W, SLC_CHUNK_ROWS), F32),
                            pltpu.SemaphoreType.DMA((2,)), pltpu.SemaphoreType.DMA((2,)),
                            pltpu.VMEM((NEW_PAD, KV_W), BF16), pltpu.VMEM((NEW_PAD, KV_W), BF16),
                            pltpu.VMEM((S_ROWS, KV_W), F32),
                            pltpu.VMEM((S_ROWS, LANES), F32), pltpu.VMEM((S_ROWS, LANES), F32)]),
        out_shape=jax.ShapeDtypeStruct((DEC_BATCH, S_ROWS, HEAD_DIM), F32),
        compiler_params=pltpu.CompilerParams(dimension_semantics=("arbitrary",), vmem_limit_bytes=VMEM_LIMIT),
        name="attn_sample",
    )(page_table, qt, ck, cv, mmat, cache_k, cache_v, ksn, vsn, win_k, win_v, kwn, vwn, gates, za)


OUT_TM = 512


def _outproj_kernel(h_ref, yc_ref, ya_ref, pe_ref, wc_ref, wa_ref, wg_ref, wp_ref, fg_ref, o_ref, *, final):
    h1 = h_ref[...] + (_dot(yc_ref[...], wc_ref[...]) + _dot(ya_ref[...], wa_ref[...]))
    h2 = h1 + _sigmoid(_dot(h1.astype(BF16), wg_ref[...])) * _dot(pe_ref[...].astype(BF16), wp_ref[...])
    if final:
        h2 = (h2 * lax.rsqrt(jnp.mean(h2 * h2, axis=-1, keepdims=True) + EPS)) * fg_ref[...]
    o_ref[...] = h2


def _outproj(h, yc, ya, pe, wc, wa, wg, wp, fg, final):
    n_rows = h.shape[0]
    tm = min(OUT_TM, n_rows)
    row = lambda w: pl.BlockSpec((tm, w), lambda i: (i, 0))
    const = lambda shape: pl.BlockSpec(shape, lambda i: (0, 0), pipeline_mode=pl.Buffered(1))
    return pl.pallas_call(
        functools.partial(_outproj_kernel, final=final),
        grid=(n_rows // tm,),
        in_specs=[row(D_MODEL), row(C_CONV), row(ATTN_W), row(PLE_DIM),
                  const((C_CONV, D_MODEL)), const((ATTN_W, D_MODEL)), const((D_MODEL, D_MODEL)),
                  const((PLE_DIM, D_MODEL)), const((1, D_MODEL))],
        out_specs=row(D_MODEL),
        out_shape=jax.ShapeDtypeStruct((n_rows, D_MODEL), F32),
        compiler_params=pltpu.CompilerParams(dimension_semantics=("parallel",), vmem_limit_bytes=VMEM_LIMIT),
        name="outproj_final" if final else "outproj",
    )(h, yc, ya, pe, wc, wa, wg, wp, fg)


def _rope_tables(pos):
    n_rows = pos.shape[0]
    half = ROT_DIM // 2
    inv = ROPE_THETA ** (-2.0 * jnp.arange(half, dtype=F32) / ROT_DIM)
    ang = pos.astype(F32)[:, None] * inv[None, :]
    cos, sin = jnp.cos(ang), jnp.sin(ang)
    one = jnp.ones((n_rows, HEAD_DIM - ROT_DIM), F32)
    zero = lambda n: jnp.zeros((n_rows, n), F32)
    c = jnp.concatenate([cos, cos, one], axis=1)
    sa = jnp.concatenate([-sin, zero(HEAD_DIM - half)], axis=1)
    sb = jnp.concatenate([zero(half), sin, zero(HEAD_DIM - ROT_DIM)], axis=1)
    return tuple(jnp.concatenate([t, t], axis=1) for t in (c, sa, sb))


def _cmp_to_sel(nsb, pad):
    n = jnp.arange(N_CMP)[:, None] * CMP_STRIDE
    j = jnp.arange(pad)[None, :]
    m = (n < (j + 1) * SEL_BLOCK) & (n + CMP_LEN > j * SEL_BLOCK) & (j < nsb)
    return m.astype(BF16)


def _expand_cmp_weight(w):
    w3 = w.reshape(CMP_LEN, HEAD_DIM, HEAD_DIM)
    eye = jnp.eye(KV_HEADS, dtype=w.dtype)
    return jnp.einsum('lde,gh->lgdhe', w3, eye).reshape(CMP_LEN, KV_W, KV_W).astype(BF16)


def _pages_transposed(cache):
    d, n, p = cache.shape[:3]
    return cache.transpose(0, 1, 3, 4, 2).reshape(d, n, KV_W, p)


def _to_head_rows(x, width):
    return x.reshape(DEC_BATCH, DEC_SEQ, N_HEADS, width).transpose(0, 2, 1, 3).reshape(DEC_BATCH, S_ROWS, width)


def kernel(x_prompt, x_sample, cache_cmp_k, cache_cmp_v, cache_slc_k, cache_slc_v, cache_win_k, cache_win_v,
           state_conv, page_table, p_prompt, p_sample, norm_g, w_in, conv_w, conv_b, conv_ln_g, conv_ln_b,
           w_cmp_k, w_cmp_v, w_out, w_ple, w_ple_gate, final_g):
    h_p, h_s = x_prompt.reshape(SEQ, D_MODEL), x_sample.reshape(N_SAMPLE, D_MODEL)
    rope_p = _rope_tables(jnp.arange(SEQ))
    rope_s = _rope_tables(PAST_LEN + jnp.tile(jnp.arange(DEC_SEQ), DEC_BATCH))
    mm_p = _cmp_to_sel(NSB_P, NSB_P)
    mm_s = _cmp_to_sel(NSB_S, NSB_S_PAD)
    cck, ccv = _pages_transposed(cache_cmp_k), _pages_transposed(cache_cmp_v)
    csk, csv = _pages_transposed(cache_slc_k), _pages_transposed(cache_slc_v)
    cwk, cwv = _pages_transposed(cache_win_k), _pages_transposed(cache_win_v)
    page_row = jnp.arange(2 * PAGE_SIZE)
    in_page = page_row % PAGE_SIZE
    perm_col = (page_row - in_page) + (in_page % CMP_STRIDE) * PAGE_CHUNKS + in_page // CMP_STRIDE
    perm = (page_row[:, None] == perm_col[None, :]).astype(BF16)
    hist_p = jnp.zeros((1, HIST_PAD, C_CONV), F32)
    hist_s = jnp.pad(state_conv, ((0, 0), (0, 0), (HIST_OFF, 0), (0, 0)))

    o_q = 3 * C_CONV
    o_kv = o_q + ATTN_W
    o_gt = o_kv + 6 * KV_W
    o_za = o_gt + 3 * N_HEADS
    kv = lambda wi, n: wi[:, o_kv + n * KV_W:o_kv + (n + 1) * KV_W]

    st_p, st_s = [], []
    for i in range(DEPTH):
        wi = w_in[i]
        w_a = jnp.concatenate([wi[:, :o_q], wi[:, o_za:]], axis=1).astype(BF16)
        w_b = jnp.concatenate([wi[:, o_q:o_kv], kv(wi, 0), kv(wi, 2), kv(wi, 4), kv(wi, 1), kv(wi, 3), kv(wi, 5)],
                              axis=1).astype(BF16)
        w_g = jnp.pad(wi[:, o_gt:o_za], ((0, 0), (0, LANES - 3 * N_HEADS))).astype(BF16)
        g_row = norm_g[i].reshape(1, D_MODEL)

        conv_args = (conv_w[i], conv_b[i].reshape(1, C_CONV), conv_ln_g[i].reshape(1, C_CONV),
                     conv_ln_b[i].reshape(1, C_CONV))
        wab_k, wab_v = _expand_cmp_weight(w_cmp_k[i]), _expand_cmp_weight(w_cmp_v[i])
        wo = w_out[i].astype(BF16)
        out_w = (wo[:C_CONV], wo[C_CONV:], w_ple_gate[i].astype(BF16), w_ple[i].astype(BF16), final_g.reshape(1, D_MODEL))
        final = i == DEPTH - 1
        seg = lambda u, n: u[:, (B_KC + n) * KV_W:(B_KC + n + 1) * KV_W]
        heads = lambda x, b, t: x.reshape(b, t, KV_HEADS, HEAD_DIM)

        ua, = _inproj(h_p, g_row, w_a)
        ub, ubb, gates = _inproj(h_p, g_row, w_b, extras=(w_g, *rope_p))
        yc, nc_p = _conv(ua, hist_p, *conv_args, batch=1, t_len=SEQ, tm=256)
        ck = _pcompress(ub, B_KC, wab_k)
        cv = _pcompress(ub, B_VC, wab_v)
        ya = _pattn(ub, ck, cv, mm_p, _pack_keys(ub, B_KS), ubb, gates, ua)
        h_p = _outproj(h_p, yc, ya, p_prompt[i].reshape(SEQ, PLE_DIM), *out_w, final=final)
        kc, ks, kw, vc, vs, vw = (seg(ub, n) for n in range(6))
        wb = min(WINDOW, SEQ)
        st_p.append((heads(kc, 1, SEQ), heads(vc, 1, SEQ), heads(ks, 1, SEQ), heads(vs, 1, SEQ),
                     heads(kw[SEQ - wb:], 1, wb), heads(vw[SEQ - wb:], 1, wb), nc_p[:, HIST_OFF:]))

        ua, = _inproj(h_s, g_row, w_a)
        ub, _, gates = _inproj(h_s, g_row, w_b, extras=(w_g, *rope_s))
        yc, nc_s = _conv(ua, hist_s[i], *conv_args, batch=DEC_BATCH, t_len=DEC_SEQ, tm=DEC_SEQ)
        kc, ks, kw, vc, vs, vw = (seg(ub, n).reshape(DEC_BATCH, DEC_SEQ, KV_W) for n in range(6))
        halves = lambda w: jnp.stack([jnp.concatenate(
            [w[s:s + CMP_STRIDE, c * LANES:(c + 1) * LANES, c * LANES:(c + 1) * LANES].reshape(CMP_STRIDE * LANES, LANES)
             for s in (0, CMP_STRIDE)], axis=1) for c in range(LANE_HALVES)])
        ck = _scompress(page_table, cck, kc, perm, halves(wab_k), i)
        cv = _scompress(page_table, ccv, vc, perm, halves(wab_v), i)
        qt = jnp.tile(_to_head_rows(ub[:, :ATTN_W], HEAD_DIM), (1, 1, KV_HEADS))
        gt = gates[:, :3 * N_HEADS].reshape(DEC_BATCH, DEC_SEQ, 3, N_HEADS).transpose(0, 3, 1, 2)
        gt = jnp.pad(gt.reshape(DEC_BATCH, S_ROWS, 3), ((0, 0), (0, 0), (0, LANES - 3)))
        za = _to_head_rows(ua[:, 3 * C_CONV:], HEAD_DIM)
        ya = _sattn(page_table, qt, ck, cv, mm_s, csk, csv, ks, vs, cwk, cwv, kw, vw, gt, za, i)
        ya = ya.reshape(DEC_BATCH, N_HEADS, DEC_SEQ, HEAD_DIM).transpose(0, 2, 1, 3).reshape(N_SAMPLE, ATTN_W)
        h_s = _outproj(h_s, yc.astype(BF16), ya.astype(BF16), p_sample[i].reshape(N_SAMPLE, PLE_DIM), *out_w, final=final)
        hs = lambda x: heads(x, DEC_BATCH, DEC_SEQ)
        st_s.append((hs(kc), hs(vc), hs(ks), hs(vs),
                     jnp.concatenate([cache_win_k[i][:, DEC_SEQ:], hs(kw)], axis=1),
                     jnp.concatenate([cache_win_v[i][:, DEC_SEQ:], hs(vw)], axis=1), nc_s[:, HIST_OFF:]))

    y_prompt = h_p.reshape(1, SEQ, D_MODEL)
    y_sample = h_s.reshape(DEC_BATCH, DEC_SEQ, D_MODEL)
    outs_p = [jnp.stack([s[n] for s in st_p]) for n in range(7)]
    outs_s = [jnp.stack([s[n] for s in st_s]) for n in range(7)]
    return (y_prompt, y_sample, *outs_p, *outs_s)
```

```python
import functools

import jax
import jax.numpy as jnp
from jax import lax
from jax.experimental import pallas as pl
from jax.experimental.pallas import tpu as pltpu

F32 = jnp.float32
BF16 = jnp.bfloat16

D_MODEL = 2048
SEQ = 8192
DEPTH = 2
DEC_BATCH = 32
DEC_SEQ = 8
PAST_LEN = 8192
PAGE_SIZE = 128
HEAD_DIM = 64
ATTN_W = D_MODEL // 2
C_CONV = D_MODEL - ATTN_W
N_HEADS = ATTN_W // HEAD_DIM
KV_HEADS = N_HEADS // 4
GQA = N_HEADS // KV_HEADS
KV_W = KV_HEADS * HEAD_DIM
ROT_DIM = HEAD_DIM // 4
ROPE_THETA = 500000.0
CONV_WIDTH = 31
CMP_LEN = 32
CMP_STRIDE = 16
SEL_BLOCK = 64
SEL_TOPK = 16
WINDOW = 512
Q_BLOCK = 128
PLE_DIM = 256
EPS = 1e-6
NEG = -1e30
BIG = 1e9

N_SAMPLE = DEC_BATCH * DEC_SEQ
N_PAGES = PAST_LEN // PAGE_SIZE
N_CHUNK = PAST_LEN // CMP_STRIDE
N_CMP = N_CHUNK
PAGE_CHUNKS = PAGE_SIZE // CMP_STRIDE
NSB_P = SEQ // SEL_BLOCK
NSB_S = -(-(PAST_LEN + DEC_SEQ) // SEL_BLOCK)
NSB_S_PAD = 256
BELOW_NEG = -3e38
LOG2E = 1.4426950408889634
SCALE = HEAD_DIM ** -0.5 * LOG2E

LANES = 128
SUBLANES = 8
VMEM_LIMIT = 56 * 1024 * 1024

A_COLS = 4 * C_CONV
B_COLS = ATTN_W + 6 * KV_W
B_ROPE_COLS = ATTN_W + 3 * KV_W
B_KC, B_KS, B_KW, B_VC, B_VS, B_VW = (ATTN_W // KV_W + n for n in range(6))
PROJ_TM = 1024
PROJ_TN = 512
SLC_KT = 1024


def _dot(a, b):
    return jnp.dot(a, b, preferred_element_type=F32)


def _dot_nt(a, b):
    return lax.dot_general(a, b, (((1,), (1,)), ((), ())), preferred_element_type=F32)


def _lane_tile(x, n):
    return x if n == 1 else jnp.concatenate([x] * n, axis=1)


def _sigmoid(x):
    return 1.0 / (1.0 + jnp.exp(-x))


def _inproj_kernel(*refs, rope_chunks, extras):
    if extras:
        x_ref, g_ref, w_ref, wg_ref, cos_ref, sa_ref, sb_ref, u_ref, ub_ref, gate_ref, hn_ref = refs
    else:
        x_ref, g_ref, w_ref, u_ref, hn_ref = refs
    j = pl.program_id(1)

    @pl.when(j == 0)
    def _():
        x = x_ref[...]
        r = lax.rsqrt(jnp.mean(x * x, axis=-1, keepdims=True) + EPS)
        hn = ((x * r) * g_ref[...]).astype(BF16)
        hn_ref[...] = hn
        if extras:
            gate_ref[...] = _dot(hn, wg_ref[...])

    acc = _dot(hn_ref[...], w_ref[...])
    if not extras:
        u_ref[...] = acc
        return

    def store(n_rope):
        c, sa, sb = cos_ref[...], sa_ref[...], sb_ref[...]
        for k in range(acc.shape[1] // LANES):
            cs = slice(k * LANES, (k + 1) * LANES)
            y = acc[:, cs]
            if k < n_rope:
                y = y * c + pltpu.roll(y, LANES - ROT_DIM // 2, 1) * sa + pltpu.roll(y, ROT_DIM // 2, 1) * sb
            u_ref[:, cs] = y
            ub_ref[:, cs] = y.astype(BF16)

    full_tiles, part_chunks = rope_chunks // (acc.shape[1] // LANES), rope_chunks % (acc.shape[1] // LANES)
    pl.when(j < full_tiles)(lambda: store(acc.shape[1] // LANES))
    if part_chunks:
        pl.when(j == full_tiles)(lambda: store(part_chunks))
    pl.when(j >= full_tiles + bool(part_chunks))(lambda: store(0))


def _inproj(x, g, w, extras=None):
    n_rows, n_cols = x.shape[0], w.shape[1]
    tm, tn = min(PROJ_TM, n_rows), PROJ_TN
    grid = (n_rows // tm, n_cols // tn)
    row = lambda i, j: (i, 0)
    in_specs = [pl.BlockSpec((tm, D_MODEL), row),
                pl.BlockSpec((1, D_MODEL), lambda i, j: (0, 0)),
                pl.BlockSpec((D_MODEL, tn), lambda i, j: (0, j))]
    out_specs = [pl.BlockSpec((tm, tn), lambda i, j: (i, j))]
    out_shape = [jax.ShapeDtypeStruct((n_rows, n_cols), F32)]
    args = [x, g, w]
    if extras:
        wg, cos, sa, sb = extras
        in_specs += [pl.BlockSpec((D_MODEL, LANES), lambda i, j: (0, 0))] + [pl.BlockSpec((tm, LANES), row)] * 3
        out_specs += [pl.BlockSpec((tm, tn), lambda i, j: (i, j)), pl.BlockSpec((tm, LANES), row)]
        out_shape += [jax.ShapeDtypeStruct((n_rows, n_cols), BF16), jax.ShapeDtypeStruct((n_rows, LANES), F32)]
        args += [wg, cos, sa, sb]
    return pl.pallas_call(
        functools.partial(_inproj_kernel, rope_chunks=B_ROPE_COLS // LANES, extras=bool(extras)),
        grid=grid, in_specs=in_specs, out_specs=out_specs, out_shape=out_shape,
        scratch_shapes=[pltpu.VMEM((tm, D_MODEL), BF16)],
        compiler_params=pltpu.CompilerParams(dimension_semantics=("parallel", "arbitrary"),
                                             vmem_limit_bytes=VMEM_LIMIT),
        name="inproj_b" if extras else "inproj_a",
    )(*args)


HIST_PAD = 32
HIST_OFF = HIST_PAD - (CONV_WIDTH - 1)
CONV_ROWS = 128
NORM_ROWS = 32


def _conv_kernel(a_ref, b_ref, zc_ref, hist_ref, w_ref, cb_ref, lg_ref, lb_ref, yc_ref, nc_ref,
                 ext_ref, sh_ref, y_ref, *, tm):
    t = pl.program_id(1)

    @pl.when(t == 0)
    def _():
        ext_ref[0:HIST_PAD, :] = hist_ref[0]

    ext_ref[HIST_PAD:HIST_PAD + tm, :] = a_ref[...] * _sigmoid(b_ref[...])
    n_sh = tm + HIST_PAD - SUBLANES
    for s in range(1, SUBLANES):
        sh_ref[s - 1] = ext_ref[s:s + n_sh, :]
    rb = min(tm, CONV_ROWS)
    for c in range(C_CONV // LANES):
        cs = slice(c * LANES, (c + 1) * LANES)
        for r0 in range(0, tm, rb):
            acc = jnp.zeros((rb, LANES), F32) + cb_ref[:, cs]
            for k in range(CONV_WIDTH):
                base, s = (HIST_OFF + k) // SUBLANES * SUBLANES + r0, (HIST_OFF + k) % SUBLANES
                rows = ext_ref[base:base + rb, cs] if s == 0 else sh_ref[s - 1, base:base + rb, cs]
                acc = acc + rows * w_ref[k:k + 1, cs]
            y_ref[r0:r0 + rb, cs] = acc
    nb = min(tm, NORM_ROWS)
    for r0 in range(0, tm, nb):
        rs = slice(r0, r0 + nb)
        y = y_ref[rs, :]
        mu = jnp.mean(y, axis=-1, keepdims=True)
        yc = y - mu
        yn = yc * lax.rsqrt(jnp.mean(yc * yc, axis=-1, keepdims=True) + EPS) * lg_ref[...] + lb_ref[...]
        zc = zc_ref[rs, :]
        yc_ref[rs, :] = ((yn * _sigmoid(yn)) * (zc * _sigmoid(zc))).astype(yc_ref.dtype)
    tail = ext_ref[tm:tm + HIST_PAD, :]
    ext_ref[0:HIST_PAD, :] = tail

    @pl.when(t == pl.num_programs(1) - 1)
    def _():
        nc_ref[0] = tail


def _conv(ua, hist, w, cb, lg, lb, *, batch, t_len, tm):
    nt = t_len // tm
    col = lambda c: (lambda b, t: (b * nt + t, c))
    vec = pl.BlockSpec((1, C_CONV), lambda b, t: (0, 0))
    return pl.pallas_call(
        functools.partial(_conv_kernel, tm=tm),
        grid=(batch, nt),
        in_specs=[pl.BlockSpec((tm, C_CONV), col(0)), pl.BlockSpec((tm, C_CONV), col(1)),
                  pl.BlockSpec((tm, C_CONV), col(2)),
                  pl.BlockSpec((1, HIST_PAD, C_CONV), lambda b, t: (b, 0, 0)),
                  pl.BlockSpec((CONV_WIDTH, C_CONV), lambda b, t: (0, 0)), vec, vec, vec],
        out_specs=[pl.BlockSpec((tm, C_CONV), lambda b, t: (b * nt + t, 0)),
                   pl.BlockSpec((1, HIST_PAD, C_CONV), lambda b, t: (b, 0, 0))],
        out_shape=[jax.ShapeDtypeStruct((batch * t_len, C_CONV), BF16 if tm % 16 == 0 else F32),
                   jax.ShapeDtypeStruct((batch, HIST_PAD, C_CONV), F32)],
        scratch_shapes=[pltpu.VMEM((HIST_PAD + tm, C_CONV), F32),
                        pltpu.VMEM((SUBLANES - 1, tm + HIST_PAD - SUBLANES, C_CONV), F32),
                        pltpu.VMEM((tm, C_CONV), F32)],
        compiler_params=pltpu.CompilerParams(dimension_semantics=("parallel", "arbitrary"),
                                             vmem_limit_bytes=VMEM_LIMIT),
        name="conv_group",
    )(ua, ua, ua, hist, w, cb, lg, lb)


CMP_ROWS = PAST_LEN + CMP_STRIDE


LANE_HALVES = KV_W // LANES


def _compress_rows(x_ref, w_ref):
    acc = None
    for l in range(CMP_LEN):
        xl = jnp.concatenate([x_ref[c, pl.ds(l, N_CMP, stride=CMP_STRIDE), :] for c in range(LANE_HALVES)], axis=1)
        part = _dot(xl.astype(BF16), w_ref[l])
        acc = part if acc is None else acc + part
    return acc


def _pcompress_kernel(x_ref, w_ref, o_ref, rows_ref):
    for c in range(LANE_HALVES):
        rows_ref[c, 0:SEQ, :] = x_ref[:, c * LANES:(c + 1) * LANES]
        rows_ref[c, SEQ:, :] = jnp.zeros((CMP_ROWS - SEQ, LANES), F32)
    o_ref[...] = _compress_rows(rows_ref, w_ref).astype(o_ref.dtype)


def _pcompress(ub, col_block, w):
    once = pl.Buffered(1)
    return pl.pallas_call(
        _pcompress_kernel,
        grid=(1,),
        in_specs=[pl.BlockSpec((SEQ, KV_W), lambda i: (0, col_block), pipeline_mode=once),
                  pl.BlockSpec((CMP_LEN, KV_W, KV_W), lambda i: (0, 0, 0), pipeline_mode=once)],
        out_specs=pl.BlockSpec((N_CMP, KV_W), lambda i: (0, 0)),
        out_shape=jax.ShapeDtypeStruct((N_CMP, KV_W), BF16),
        scratch_shapes=[pltpu.VMEM((LANE_HALVES, CMP_ROWS, LANES), F32)],
        compiler_params=pltpu.CompilerParams(dimension_semantics=("arbitrary",), vmem_limit_bytes=VMEM_LIMIT),
        name="compress_prompt",
    )(ub, w)


S_CHUNKS = N_CHUNK + SUBLANES


def _scompress_kernel(pt_ref, cache_ref, new_ref, perm_ref, wab_ref, o_ref, tbuf, xc_ref, sem, *, layer):
    b = pl.program_id(0)
    slot = b % 2

    def copies(bb, sl, go):
        def body(p, carry):
            cp = pltpu.make_async_copy(cache_ref.at[layer, pt_ref[bb, p]],
                                       tbuf.at[sl, :, pl.ds(pl.multiple_of(p * PAGE_SIZE, PAGE_SIZE), PAGE_SIZE)],
                                       sem.at[sl])
            cp.start() if go else cp.wait()
            return carry
        lax.fori_loop(0, N_PAGES, body, 0)

    @pl.when(b == 0)
    def _():
        copies(0, 0, True)

    @pl.when(b + 1 < pl.num_programs(0))
    def _():
        copies(b + 1, 1 - slot, True)

    copies(b, slot, False)
    perm = perm_ref[...]
    for pp in range(N_PAGES // 2):
        pages = tbuf[slot, :, pp * 2 * PAGE_SIZE:(pp + 1) * 2 * PAGE_SIZE].T.astype(BF16)
        rows = _dot(perm, pages)
        for k in range(2):
            p = 2 * pp + k
            for l in range(CMP_STRIDE):
                r0 = k * PAGE_SIZE + l * PAGE_CHUNKS
                for c in range(LANE_HALVES):
                    xc_ref[c, p * PAGE_CHUNKS:(p + 1) * PAGE_CHUNKS, l * LANES:(l + 1) * LANES] = (
                        rows[r0:r0 + PAGE_CHUNKS, c * LANES:(c + 1) * LANES])
    for c in range(LANE_HALVES):
        cs = slice(c * LANES, (c + 1) * LANES)
        xc_ref[c, N_CHUNK:, :] = jnp.zeros((SUBLANES, CMP_STRIDE * LANES), F32)
        for l in range(DEC_SEQ):
            xc_ref[c, N_CHUNK:N_CHUNK + 1, l * LANES:(l + 1) * LANES] = new_ref[0, l:l + 1, cs]
        z = _dot(xc_ref[c].astype(BF16), wab_ref[c])
        zb = pltpu.roll(z[:, LANES:], S_CHUNKS - 1, 0)
        o_ref[0, :, cs] = (z[0:N_CMP, :LANES] + zb[0:N_CMP]).astype(o_ref.dtype)


def _scompress(page_table, cache_t, new_rows, perm, wab, layer):
    once = pl.Buffered(1)
    return pl.pallas_call(
        functools.partial(_scompress_kernel, layer=layer),
        grid_spec=pltpu.PrefetchScalarGridSpec(
            num_scalar_prefetch=1, grid=(DEC_BATCH,),
            in_specs=[pl.BlockSpec(memory_space=pl.ANY),
                      pl.BlockSpec((1, DEC_SEQ, KV_W), lambda b, pt: (b, 0, 0)),
                      pl.BlockSpec((2 * PAGE_SIZE, 2 * PAGE_SIZE), lambda b, pt: (0, 0), pipeline_mode=once),
                      pl.BlockSpec((LANE_HALVES, CMP_STRIDE * LANES, 2 * LANES), lambda b, pt: (0, 0, 0),
                                   pipeline_mode=once)],
            out_specs=pl.BlockSpec((1, N_CMP, KV_W), lambda b, pt: (b, 0, 0)),
            scratch_shapes=[pltpu.VMEM((2, KV_W, PAST_LEN), F32),
                            pltpu.VMEM((LANE_HALVES, S_CHUNKS, CMP_STRIDE * LANES), F32),
                            pltpu.SemaphoreType.DMA((2,))]),
        out_shape=jax.ShapeDtypeStruct((DEC_BATCH, N_CMP, KV_W), BF16),
        compiler_params=pltpu.CompilerParams(dimension_semantics=("arbitrary",), vmem_limit_bytes=VMEM_LIMIT),
        name="compress_sample",
    )(page_table, cache_t, new_rows, perm, wab)


def _cmp_branch(q, tcol, ck, cv):
    s = _dot_nt(q, ck)
    n = lax.broadcasted_iota(jnp.int32, s.shape, 1)
    vis = (n * CMP_STRIDE + (CMP_LEN - 1)) <= tcol
    sm = jnp.where(vis, s, NEG)
    e = jnp.where(vis, jnp.exp2(sm - jnp.max(sm, axis=1, keepdims=True)), 0.0)
    den = jnp.sum(e, axis=1, keepdims=True)
    p = e / jnp.where(den > 0.0, den, 1.0)
    return _dot(p.astype(BF16), cv), p


def _select_bias(imp, t, nsb, axis):
    j = lax.broadcasted_iota(jnp.int32, imp.shape, axis)
    cur = lax.shift_right_arithmetic(t, 6)
    forced = (j == 0) | (j == cur) | (j == cur - 1)
    score = jnp.where(j * SEL_BLOCK <= t, jnp.where(forced, BIG, imp), NEG)
    score = jnp.where(j < nsb, score, BELOW_NEG)
    jf = j.astype(F32)
    bias = jnp.full(imp.shape, NEG, F32)
    for _ in range(SEL_TOPK):
        mx = jnp.max(score, axis=axis, keepdims=True)
        idx = jnp.min(jnp.where(score == mx, jf, 1e9), axis=axis, keepdims=True)
        pick = jf == idx
        bias = jnp.where(pick, 0.0, bias)
        score = jnp.where(pick, BELOW_NEG, score)
    return bias


def _block_onehot(nb, kt, first_block):
    j = lax.broadcasted_iota(jnp.int32, (nb, kt), 0)
    col = lax.broadcasted_iota(jnp.int32, (nb, kt), 1)
    return jnp.where(j - lax.shift_right_arithmetic(col, 6) == first_block, 1.0, 0.0).astype(BF16)


def _online_update(s, v, m_ref, l_ref, acc_ref, rows, v_transposed=False):
    m_prev, l_prev = m_ref[rows, :], l_ref[rows, :]
    m_next = jnp.maximum(m_prev, jnp.max(s, axis=1, keepdims=True))
    alpha = jnp.exp2(m_prev - m_next)
    p = jnp.exp2(s - _lane_tile(m_next, s.shape[1] // LANES))
    l_ref[rows, :] = alpha * l_prev + jnp.sum(p, axis=1, keepdims=True)
    m_ref[rows, :] = m_next
    pv = _dot_nt(p.astype(BF16), v) if v_transposed else _dot(p.astype(BF16), v)
    acc_ref[rows, :] = acc_ref[rows, :] * _lane_tile(alpha, KV_W // LANES) + pv


def _window_mask(tcol, kpos):
    return lax.bitcast_convert_type(tcol - kpos, jnp.uint32) <= jnp.uint32(WINDOW)


GROUP_ROWS = GQA * Q_BLOCK


PACK_TM = 512


def _pack_keys_kernel(k_ref, o_ref):
    i = pl.program_id(0)
    lane = lax.broadcasted_iota(jnp.int32, (PACK_TM, LANES), 1)
    blk = lax.shift_right_arithmetic(i * PACK_TM + lax.broadcasted_iota(jnp.int32, (PACK_TM, LANES), 0), 6)
    lo = lane < HEAD_DIM
    hot_lo = jnp.where(lane - HEAD_DIM == blk, 1.0, 0.0)
    hot_hi = jnp.where(lane + HEAD_DIM == blk, 1.0, 0.0)
    for g in range(KV_HEADS):
        pair = k_ref[:, (g // 2) * LANES:(g // 2 + 1) * LANES]
        if g % 2:
            pair = pltpu.roll(pair, HEAD_DIM, 1)
        o_ref[g, :, 0:LANES] = jnp.where(lo, pair, hot_lo).astype(BF16)
        o_ref[g, :, LANES:2 * LANES] = jnp.where(lo, hot_hi, 0.0).astype(BF16)


def _pack_keys(ub, col_block):
    assert NSB_P == 2 * HEAD_DIM and KV_W == 2 * LANES
    return pl.pallas_call(
        _pack_keys_kernel,
        grid=(SEQ // PACK_TM,),
        in_specs=[pl.BlockSpec((PACK_TM, KV_W), lambda i: (i, col_block))],
        out_specs=pl.BlockSpec((KV_HEADS, PACK_TM, KV_W), lambda i: (0, i, 0)),
        out_shape=jax.ShapeDtypeStruct((KV_HEADS, SEQ, KV_W), BF16),
        compiler_params=pltpu.CompilerParams(dimension_semantics=("parallel",), vmem_limit_bytes=VMEM_LIMIT),
        name="pack_keys",
    )(ub)


def _pattn_kernel(q_ref, ck_ref, cv_ref, mm_ref, ka_ref, vs_ref, kw_ref, vw_ref, gt_ref, za_ref, o_ref,
                  qbd_ref, qa_ref, oc_ref, acc_ref, m_ref, l_ref):
    i = pl.program_id(0)
    q0 = i * Q_BLOCK
    T, R = Q_BLOCK, GROUP_ROWS
    lane = lax.broadcasted_iota(jnp.int32, (T, LANES), 1)
    half = lax.shift_right_arithmetic(lane, 6)
    lo = lane < HEAD_DIM
    tcol = q0 + (lax.broadcasted_iota(jnp.int32, (R, 1), 0) & (T - 1))
    trow = q0 + lax.broadcasted_iota(jnp.int32, (1, T), 1)

    zeros = jnp.zeros((T, LANES), BF16)
    for hh in range(N_HEADS):
        g = hh // GQA
        pair = q_ref[:, (hh // 2) * LANES:(hh // 2 + 1) * LANES] * SCALE
        if hh % 2 != g % 2:
            pair = pltpu.roll(pair, HEAD_DIM, 1)
        x = jnp.where(half == g % 2, pair, 0.0).astype(BF16)
        qbd_ref[hh * T:(hh + 1) * T, 0:LANES] = x if g // 2 == 0 else zeros
        qbd_ref[hh * T:(hh + 1) * T, LANES:2 * LANES] = zeros if g // 2 == 0 else x

    for g in range(KV_HEADS):
        rows = slice(g * R, (g + 1) * R)
        o_c, p = _cmp_branch(qbd_ref[rows, :], tcol, ck_ref[...], cv_ref[...])
        oc_ref[rows, :] = o_c
        imp_rows = _dot(p.astype(BF16), mm_ref[...])
        imp = imp_rows[0:T] + imp_rows[T:2 * T] + imp_rows[2 * T:3 * T] + imp_rows[3 * T:4 * T]
        bias = _select_bias(imp.T, trow, NSB_P, axis=0).T
        bias = pltpu.roll(bias, HEAD_DIM, 1)
        for h in range(GQA):
            hh = g * GQA + h
            pair = q_ref[:, (hh // 2) * LANES:(hh // 2 + 1) * LANES] * SCALE
            if hh % 2:
                pair = pltpu.roll(pair, HEAD_DIM, 1)
            qa_ref[hh * T:(hh + 1) * T, 0:LANES] = jnp.where(lo, pair, bias).astype(BF16)
            qa_ref[hh * T:(hh + 1) * T, LANES:2 * LANES] = jnp.where(lo, bias, 0.0).astype(BF16)

    m_ref[...] = jnp.full(m_ref.shape, NEG, F32)
    l_ref[...] = jnp.zeros(l_ref.shape, F32)
    acc_ref[...] = jnp.zeros(acc_ref.shape, F32)
    kcol = lax.broadcasted_iota(jnp.int32, (1, SLC_KT), 1)

    def slc_tile(c, causal):
        k0 = pl.multiple_of(c * SLC_KT, SLC_KT)
        v_t = vs_ref[pl.ds(k0, SLC_KT), :]
        for g in range(KV_HEADS):
            rows = slice(g * R, (g + 1) * R)
            s = _dot_nt(qa_ref[rows, :], ka_ref[g, pl.ds(k0, SLC_KT), :])
            if causal:
                s = jnp.where(k0 + kcol <= tcol, s, NEG)
            _online_update(s, v_t, m_ref, l_ref, acc_ref, rows)

    c_last = (q0 + T - 1) // SLC_KT

    def full_tile(c, carry):
        slc_tile(c, False)
        return carry

    lax.fori_loop(0, c_last, full_tile, 0)
    slc_tile(c_last, True)

    start = pl.multiple_of(jnp.maximum(q0 - WINDOW, 0), Q_BLOCK)
    kw_t = kw_ref[pl.ds(start, WINDOW + T), :]
    vw_t = vw_ref[pl.ds(start, WINDOW + T), :]
    kpos = start + lax.broadcasted_iota(jnp.int32, (1, WINDOW + T), 1)
    sig = _sigmoid(gt_ref[...])
    for g in range(KV_HEADS):
        rows = slice(g * R, (g + 1) * R)
        cc = slice((g // 2) * LANES, (g // 2 + 1) * LANES)
        s3 = jnp.where(_window_mask(tcol, kpos), _dot_nt(qbd_ref[rows, :], kw_t), NEG)
        e3 = jnp.exp2(s3 - jnp.max(s3, axis=1, keepdims=True))
        o_w = _dot(e3.astype(BF16), vw_t)[:, cc] / jnp.sum(e3, axis=1, keepdims=True)
        o_s = acc_ref[rows, cc] / l_ref[rows, :]
        o_c = oc_ref[rows, cc]
        for pair in range(2):
            pieces = []
            for sub in range(2):
                h = 2 * pair + sub
                hh = g * GQA + h
                hr = slice(h * T, (h + 1) * T)
                comb = (sig[:, hh:hh + 1] * o_c[hr] + sig[:, N_HEADS + hh:N_HEADS + hh + 1] * o_s[hr]
                        + sig[:, 2 * N_HEADS + hh:2 * N_HEADS + hh + 1] * o_w[hr])
                pieces.append(comb if g % 2 == sub else pltpu.roll(comb, HEAD_DIM, 1))
            oc_cols = slice((2 * g + pair) * LANES, (2 * g + pair + 1) * LANES)
            za = za_ref[:, oc_cols]
            o_ref[:, oc_cols] = (jnp.where(half == 0, pieces[0], pieces[1]) * (za * _sigmoid(za))).astype(BF16)


def _pattn(ub, ck, cv, mmat, kaug, ubb, gates, ua):
    const = lambda r, c: (lambda i: (r, c))
    once = pl.Buffered(1)
    resident = lambda c: pl.BlockSpec((SEQ, KV_W), const(0, c), pipeline_mode=once)
    small = lambda shape: pl.BlockSpec(shape, const(0, 0))
    return pl.pallas_call(
        _pattn_kernel,
        grid=(SEQ // Q_BLOCK,),
        in_specs=[pl.BlockSpec((Q_BLOCK, ATTN_W), lambda i: (i, 0)),
                  small((N_CMP, KV_W)), small((N_CMP, KV_W)), small((N_CMP, NSB_P)),
                  pl.BlockSpec((KV_HEADS, SEQ, KV_W), lambda i: (0, 0, 0), pipeline_mode=once),
                  resident(B_VS), resident(B_KW), resident(B_VW),
                  pl.BlockSpec((Q_BLOCK, LANES), lambda i: (i, 0)),
                  pl.BlockSpec((Q_BLOCK, ATTN_W), lambda i: (i, 3))],
        out_specs=pl.BlockSpec((Q_BLOCK, ATTN_W), lambda i: (i, 0)),
        out_shape=jax.ShapeDtypeStruct((SEQ, ATTN_W), BF16),
        scratch_shapes=[pltpu.VMEM((N_HEADS * Q_BLOCK, KV_W), BF16),
                        pltpu.VMEM((N_HEADS * Q_BLOCK, KV_W), BF16),
                        pltpu.VMEM((N_HEADS * Q_BLOCK, KV_W), F32),
                        pltpu.VMEM((N_HEADS * Q_BLOCK, KV_W), F32),
                        pltpu.VMEM((N_HEADS * Q_BLOCK, LANES), F32),
                        pltpu.VMEM((N_HEADS * Q_BLOCK, LANES), F32)],
        compiler_params=pltpu.CompilerParams(dimension_semantics=("parallel",), vmem_limit_bytes=VMEM_LIMIT),
        name="attn_prompt",
    )(ub, ck, cv, mmat, kaug, ubb, ubb, ubb, gates, ua)


SLC_PAGES = 16
SLC_CHUNKS = N_PAGES // SLC_PAGES
SLC_CHUNK_ROWS = SLC_PAGES * PAGE_SIZE
S_ROWS = N_HEADS * DEC_SEQ
NEW_PAD = LANES


def _sattn_kernel(pt_ref, qt_ref, ck_ref, cv_ref, mm_ref, ksc_ref, vsc_ref, ksn_ref, vsn_ref,
                  kwc_ref, vwc_ref, kwn_ref, vwn_ref, gt_ref, za_ref, o_ref,
                  kbuf, vbuf, ksem, vsem, newk_ref, newv_ref, s_ref, *, layer):
    b = pl.program_id(0)
    nb = pl.num_programs(0)
    R = S_ROWS
    row = lax.broadcasted_iota(jnp.int32, (R, 1), 0)
    tcol = PAST_LEN + (row & (DEC_SEQ - 1))
    lane_group = lax.shift_right_arithmetic(lax.broadcasted_iota(jnp.int32, (R, KV_W), 1), 6)
    diag = lane_group == lax.shift_right_arithmetic(row, 5)
    vslot = b % 2

    def page_copies(cache, buf, sem, bb, first_page, n_pages, slot, go):
        def body(p, carry):
            cp = pltpu.make_async_copy(cache.at[layer, pt_ref[bb, first_page + p]],
                                       buf.at[slot, :, pl.ds(pl.multiple_of(p * PAGE_SIZE, PAGE_SIZE), PAGE_SIZE)],
                                       sem.at[slot])
            cp.start() if go else cp.wait()
            return carry
        lax.fori_loop(0, n_pages, body, 0)

    def key_chunk(bb, c, slot, go):
        page_copies(ksc_ref, kbuf, ksem, bb, c * SLC_PAGES, SLC_PAGES, slot, go)

    def value_pages(bb, slot, go):
        page_copies(vsc_ref, vbuf, vsem, bb, 0, N_PAGES, slot, go)

    @pl.when(b == 0)
    def _():
        key_chunk(0, 0, 0, True)
        value_pages(0, 0, True)

    @pl.when(b + 1 < nb)
    def _():
        value_pages(b + 1, 1 - vslot, True)

    q = jnp.where(diag, qt_ref[0] * SCALE, 0.0).astype(BF16)

    o_c, p = _cmp_branch(q, tcol, ck_ref[0], cv_ref[0])
    T = DEC_SEQ
    imp_rows = _dot(p.astype(BF16), mm_ref[...])
    groups = []
    for r0 in range(0, R, GQA * T):
        grp = imp_rows[r0:r0 + T] + imp_rows[r0 + T:r0 + 2 * T] + imp_rows[r0 + 2 * T:r0 + 3 * T] + imp_rows[r0 + 3 * T:r0 + 4 * T]
        groups += [grp] * GQA
    imp = jnp.concatenate(groups, axis=0)
    trow = PAST_LEN + (lax.broadcasted_iota(jnp.int32, (1, R), 1) & (T - 1))
    a = _select_bias(imp.T, trow, NSB_S, axis=0).T.astype(BF16)

    def lane_blocks(x):
        return [x[:, j * LANES:(j + 1) * LANES] for j in range(x.shape[1] // LANES)]

    m_lanes = jnp.full((R, LANES), NEG, F32)
    for c in range(SLC_CHUNKS):
        slot = c % 2
        if c + 1 < SLC_CHUNKS:
            key_chunk(b, c + 1, 1 - slot, True)
        else:
            @pl.when(b + 1 < nb)
            def _():
                key_chunk(b + 1, 0, 1 - slot, True)
        key_chunk(b, c, slot, False)
        onehot = _block_onehot(NSB_S_PAD, SLC_CHUNK_ROWS, c * SLC_CHUNK_ROWS // SEL_BLOCK)
        s = _dot(q, kbuf[slot].astype(BF16)) + _dot(a, onehot)
        s_ref[:, c * SLC_CHUNK_ROWS:(c + 1) * SLC_CHUNK_ROWS] = s
        for blk in lane_blocks(s):
            m_lanes = jnp.maximum(m_lanes, blk)

    newcol = lax.broadcasted_iota(jnp.int32, (1, NEW_PAD), 1)
    new_ok = PAST_LEN + newcol <= tcol
    newk_ref[...] = jnp.zeros(newk_ref.shape, BF16)
    newv_ref[...] = jnp.zeros(newv_ref.shape, BF16)
    newk_ref[0:T, :] = ksn_ref[0].astype(BF16)
    newv_ref[0:T, :] = vsn_ref[0].astype(BF16)
    onehot = _block_onehot(NSB_S_PAD, NEW_PAD, PAST_LEN // SEL_BLOCK)
    s_new = jnp.where(new_ok, _dot_nt(q, newk_ref[...]) + _dot(a, onehot), NEG)
    m = jnp.max(jnp.maximum(m_lanes, s_new), axis=1, keepdims=True)

    value_pages(b, vslot, False)
    p_new = jnp.exp2(s_new - m)
    l_lanes = p_new
    acc = _dot(p_new.astype(BF16), newv_ref[...])
    for c in range(SLC_CHUNKS):
        cs = slice(c * SLC_CHUNK_ROWS, (c + 1) * SLC_CHUNK_ROWS)
        p = jnp.exp2(s_ref[:, cs] - m)
        for blk in lane_blocks(p):
            l_lanes = l_lanes + blk
        acc = acc + _dot_nt(p.astype(BF16), vbuf[vslot, :, cs].astype(BF16))
    o_s = acc / jnp.sum(l_lanes, axis=1, keepdims=True)

    wpos = (PAST_LEN - WINDOW) + lax.broadcasted_iota(jnp.int32, (1, WINDOW), 1)
    newk_ref[0:T, :] = kwn_ref[0].astype(BF16)
    newv_ref[0:T, :] = vwn_ref[0].astype(BF16)
    s3 = jnp.concatenate(
        [jnp.where(_window_mask(tcol, wpos), _dot(q, kwc_ref[0, 0].astype(BF16)), NEG),
         jnp.where(new_ok, _dot_nt(q, newk_ref[...]), NEG)], axis=1)
    e3 = jnp.exp2(s3 - jnp.max(s3, axis=1, keepdims=True))
    p3 = e3.astype(BF16)
    o_w = (_dot_nt(p3[:, 0:WINDOW], vwc_ref[0, 0].astype(BF16)) + _dot(p3[:, WINDOW:], newv_ref[...]))
    o_w = o_w / jnp.sum(e3, axis=1, keepdims=True)

    sig = _sigmoid(gt_ref[0])
    comb = jnp.where(diag, sig[:, 0:1] * o_c + sig[:, 1:2] * o_s + sig[:, 2:3] * o_w, 0.0)
    folded = comb[:, 0:LANES] + comb[:, LANES:2 * LANES]
    folded = folded + pltpu.roll(folded, HEAD_DIM, 1)
    za = za_ref[0]
    o_ref[0] = folded[:, 0:HEAD_DIM] * (za * _sigmoid(za))


def _sattn(page_table, qt, ck, cv, mmat, cache_k, cache_v, ksn, vsn, win_k, win_v, kwn, vwn, gates, za, layer):
    per_b = lambda shape: pl.BlockSpec((1,) + shape, lambda b, pt: (b,) + (0,) * len(shape))
    win = pl.BlockSpec((1, 1, KV_W, WINDOW), lambda b, pt: (layer, b, 0, 0))
    hbm = pl.BlockSpec(memory_space=pl.ANY)
    return pl.pallas_call(
        functools.partial(_sattn_kernel, layer=layer),
        grid_spec=pltpu.PrefetchScalarGridSpec(
            num_scalar_prefetch=1, grid=(DEC_BATCH,),
            in_specs=[per_b((S_ROWS, KV_W)), per_b((N_CMP, KV_W)), per_b((N_CMP, KV_W)),
                      pl.BlockSpec((N_CMP, NSB_S_PAD), lambda b, pt: (0, 0)),
                      hbm, hbm, per_b((DEC_SEQ, KV_W)), per_b((DEC_SEQ, KV_W)),
                      win, win, per_b((DEC_SEQ, KV_W)), per_b((DEC_SEQ, KV_W)),
                      per_b((S_ROWS, LANES)), per_b((S_ROWS, HEAD_DIM))],
            out_specs=per_b((S_ROWS, HEAD_DIM)),
            scratch_shapes=[pltpu.VMEM((2, KV_W, SLC_CHUNK_ROWS), F32), pltpu.VMEM((2, KV_W, PAST_LEN), F32),
                            pltpu.SemaphoreType.DMA((2,)), pltpu.SemaphoreType.DMA((2,)),
                            pltpu.VMEM((NEW_PAD, KV_W), BF16), pltpu.VMEM((NEW_PAD, KV_W), BF16),
                            pltpu.VMEM((S_ROWS, PAST_LEN), F32)]),
        out_shape=jax.ShapeDtypeStruct((DEC_BATCH, S_ROWS, HEAD_DIM), F32),
        compiler_params=pltpu.CompilerParams(dimension_semantics=("arbitrary",), vmem_limit_bytes=VMEM_LIMIT),
        name="attn_sample",
    )(page_table, qt, ck, cv, mmat, cache_k, cache_v, ksn, vsn, win_k, win_v, kwn, vwn, gates, za)


OUT_TM = 512


def _outproj_kernel(h_ref, yc_ref, ya_ref, pe_ref, wc_ref, wa_ref, wg_ref, wp_ref, fg_ref, o_ref, *, final):
    h1 = h_ref[...] + (_dot(yc_ref[...], wc_ref[...]) + _dot(ya_ref[...], wa_ref[...]))
    h2 = h1 + _sigmoid(_dot(h1.astype(BF16), wg_ref[...])) * _dot(pe_ref[...].astype(BF16), wp_ref[...])
    if final:
        h2 = (h2 * lax.rsqrt(jnp.mean(h2 * h2, axis=-1, keepdims=True) + EPS)) * fg_ref[...]
    o_ref[...] = h2


def _outproj(h, yc, ya, pe, wc, wa, wg, wp, fg, final):
    n_rows = h.shape[0]
    tm = min(OUT_TM, n_rows)
    row = lambda w: pl.BlockSpec((tm, w), lambda i: (i, 0))
    const = lambda shape: pl.BlockSpec(shape, lambda i: (0, 0), pipeline_mode=pl.Buffered(1))
    return pl.pallas_call(
        functools.partial(_outproj_kernel, final=final),
        grid=(n_rows // tm,),
        in_specs=[row(D_MODEL), row(C_CONV), row(ATTN_W), row(PLE_DIM),
                  const((C_CONV, D_MODEL)), const((ATTN_W, D_MODEL)), const((D_MODEL, D_MODEL)),
                  const((PLE_DIM, D_MODEL)), const((1, D_MODEL))],
        out_specs=row(D_MODEL),
        out_shape=jax.ShapeDtypeStruct((n_rows, D_MODEL), F32),
        compiler_params=pltpu.CompilerParams(dimension_semantics=("parallel",), vmem_limit_bytes=VMEM_LIMIT),
        name="outproj_final" if final else "outproj",
    )(h, yc, ya, pe, wc, wa, wg, wp, fg)


def _rope_tables(pos):
    n_rows = pos.shape[0]
    half = ROT_DIM // 2
    inv = ROPE_THETA ** (-2.0 * jnp.arange(half, dtype=F32) / ROT_DIM)
    ang = pos.astype(F32)[:, None] * inv[None, :]
    cos, sin = jnp.cos(ang), jnp.sin(ang)
    one = jnp.ones((n_rows, HEAD_DIM - ROT_DIM), F32)
    zero = lambda n: jnp.zeros((n_rows, n), F32)
    c = jnp.concatenate([cos, cos, one], axis=1)
    sa = jnp.concatenate([-sin, zero(HEAD_DIM - half)], axis=1)
    sb = jnp.concatenate([zero(half), sin, zero(HEAD_DIM - ROT_DIM)], axis=1)
    return tuple(jnp.concatenate([t, t], axis=1) for t in (c, sa, sb))


def _cmp_to_sel(nsb, pad):
    n = jnp.arange(N_CMP)[:, None] * CMP_STRIDE
    j = jnp.arange(pad)[None, :]
    m = (n < (j + 1) * SEL_BLOCK) & (n + CMP_LEN > j * SEL_BLOCK) & (j < nsb)
    return m.astype(BF16)


def _expand_cmp_weight(w):
    w3 = w.reshape(CMP_LEN, HEAD_DIM, HEAD_DIM)
    eye = jnp.eye(KV_HEADS, dtype=w.dtype)
    return jnp.einsum('lde,gh->lgdhe', w3, eye).reshape(CMP_LEN, KV_W, KV_W).astype(BF16)


def _pages_transposed(cache):
    d, n, p = cache.shape[:3]
    return cache.transpose(0, 1, 3, 4, 2).reshape(d, n, KV_W, p)


def _to_head_rows(x, width):
    return x.reshape(DEC_BATCH, DEC_SEQ, N_HEADS, width).transpose(0, 2, 1, 3).reshape(DEC_BATCH, S_ROWS, width)


def kernel(x_prompt, x_sample, cache_cmp_k, cache_cmp_v, cache_slc_k, cache_slc_v, cache_win_k, cache_win_v,
           state_conv, page_table, p_prompt, p_sample, norm_g, w_in, conv_w, conv_b, conv_ln_g, conv_ln_b,
           w_cmp_k, w_cmp_v, w_out, w_ple, w_ple_gate, final_g):
    h_p, h_s = x_prompt.reshape(SEQ, D_MODEL), x_sample.reshape(N_SAMPLE, D_MODEL)
    rope_p = _rope_tables(jnp.arange(SEQ))
    rope_s = _rope_tables(PAST_LEN + jnp.tile(jnp.arange(DEC_SEQ), DEC_BATCH))
    mm_p = _cmp_to_sel(NSB_P, NSB_P)
    mm_s = _cmp_to_sel(NSB_S, NSB_S_PAD)
    cck, ccv = _pages_transposed(cache_cmp_k), _pages_transposed(cache_cmp_v)
    csk, csv = _pages_transposed(cache_slc_k), _pages_transposed(cache_slc_v)
    cwk, cwv = _pages_transposed(cache_win_k), _pages_transposed(cache_win_v)
    page_row = jnp.arange(2 * PAGE_SIZE)
    in_page = page_row % PAGE_SIZE
    perm_col = (page_row - in_page) + (in_page % CMP_STRIDE) * PAGE_CHUNKS + in_page // CMP_STRIDE
    perm = (page_row[:, None] == perm_col[None, :]).astype(BF16)
    hist_p = jnp.zeros((1, HIST_PAD, C_CONV), F32)
    hist_s = jnp.pad(state_conv, ((0, 0), (0, 0), (HIST_OFF, 0), (0, 0)))

    o_q = 3 * C_CONV
    o_kv = o_q + ATTN_W
    o_gt = o_kv + 6 * KV_W
    o_za = o_gt + 3 * N_HEADS
    kv = lambda wi, n: wi[:, o_kv + n * KV_W:o_kv + (n + 1) * KV_W]

    st_p, st_s = [], []
    for i in range(DEPTH):
        wi = w_in[i]
        w_a = jnp.concatenate([wi[:, :o_q], wi[:, o_za:]], axis=1).astype(BF16)
        w_b = jnp.concatenate([wi[:, o_q:o_kv], kv(wi, 0), kv(wi, 2), kv(wi, 4), kv(wi, 1), kv(wi, 3), kv(wi, 5)],
                              axis=1).astype(BF16)
        w_g = jnp.pad(wi[:, o_gt:o_za], ((0, 0), (0, LANES - 3 * N_HEADS))).astype(BF16)
        g_row = norm_g[i].reshape(1, D_MODEL)

        conv_args = (conv_w[i], conv_b[i].reshape(1, C_CONV), conv_ln_g[i].reshape(1, C_CONV),
                     conv_ln_b[i].reshape(1, C_CONV))
        wab_k, wab_v = _expand_cmp_weight(w_cmp_k[i]), _expand_cmp_weight(w_cmp_v[i])
        wo = w_out[i].astype(BF16)
        out_w = (wo[:C_CONV], wo[C_CONV:], w_ple_gate[i].astype(BF16), w_ple[i].astype(BF16), final_g.reshape(1, D_MODEL))
        final = i == DEPTH - 1
        seg = lambda u, n: u[:, (B_KC + n) * KV_W:(B_KC + n + 1) * KV_W]
        heads = lambda x, b, t: x.reshape(b, t, KV_HEADS, HEAD_DIM)

        ua, = _inproj(h_p, g_row, w_a)
        ub, ubb, gates = _inproj(h_p, g_row, w_b, extras=(w_g, *rope_p))
        yc, nc_p = _conv(ua, hist_p, *conv_args, batch=1, t_len=SEQ, tm=256)
        ck = _pcompress(ub, B_KC, wab_k)
        cv = _pcompress(ub, B_VC, wab_v)
        ya = _pattn(ub, ck, cv, mm_p, _pack_keys(ub, B_KS), ubb, gates, ua)
        h_p = _outproj(h_p, yc, ya, p_prompt[i].reshape(SEQ, PLE_DIM), *out_w, final=final)
        kc, ks, kw, vc, vs, vw = (seg(ub, n) for n in range(6))
        wb = min(WINDOW, SEQ)
        st_p.append((heads(kc, 1, SEQ), heads(vc, 1, SEQ), heads(ks, 1, SEQ), heads(vs, 1, SEQ),
                     heads(kw[SEQ - wb:], 1, wb), heads(vw[SEQ - wb:], 1, wb), nc_p[:, HIST_OFF:]))

        ua, = _inproj(h_s, g_row, w_a)
        ub, _, gates = _inproj(h_s, g_row, w_b, extras=(w_g, *rope_s))
        yc, nc_s = _conv(ua, hist_s[i], *conv_args, batch=DEC_BATCH, t_len=DEC_SEQ, tm=DEC_SEQ)
        kc, ks, kw, vc, vs, vw = (seg(ub, n).reshape(DEC_BATCH, DEC_SEQ, KV_W) for n in range(6))
        halves = lambda w: jnp.stack([jnp.concatenate(
            [w[s:s + CMP_STRIDE, c * LANES:(c + 1) * LANES, c * LANES:(c + 1) * LANES].reshape(CMP_STRIDE * LANES, LANES)
             for s in (0, CMP_STRIDE)], axis=1) for c in range(LANE_HALVES)])
        ck = _scompress(page_table, cck, kc, perm, halves(wab_k), i)
        cv = _scompress(page_table, ccv, vc, perm, halves(wab_v), i)
        qt = jnp.tile(_to_head_rows(ub[:, :ATTN_W], HEAD_DIM), (1, 1, KV_HEADS))
        gt = gates[:, :3 * N_HEADS].reshape(DEC_BATCH, DEC_SEQ, 3, N_HEADS).transpose(0, 3, 1, 2)
        gt = jnp.pad(gt.reshape(DEC_BATCH, S_ROWS, 3), ((0, 0), (0, 0), (0, LANES - 3)))
        za = _to_head_rows(ua[:, 3 * C_CONV:], HEAD_DIM)
        ya = _sattn(page_table, qt, ck, cv, mm_s, csk, csv, ks, vs, cwk, cwv, kw, vw, gt, za, i)
        ya = ya.reshape(DEC_BATCH, N_HEADS, DEC_SEQ, HEAD_DIM).transpose(0, 2, 1, 3).reshape(N_SAMPLE, ATTN_W)
        h_s = _outproj(h_s, yc.astype(BF16), ya.astype(BF16), p_sample[i].reshape(N_SAMPLE, PLE_DIM), *out_w, final=final)
        hs = lambda x: heads(x, DEC_BATCH, DEC_SEQ)
        st_s.append((hs(kc), hs(vc), hs(ks), hs(vs),
                     jnp.concatenate([cache_win_k[i][:, DEC_SEQ:], hs(kw)], axis=1),
                     jnp.concatenate([cache_win_v[i][:, DEC_SEQ:], hs(vw)], axis=1), nc_s[:, HIST_OFF:]))

    y_prompt = h_p.reshape(1, SEQ, D_MODEL)
    y_sample = h_s.reshape(DEC_BATCH, DEC_SEQ, D_MODEL)
    outs_p = [jnp.stack([s[n] for s in st_p]) for n in range(7)]
    outs_s = [jnp.stack([s[n] for s in st_s]) for n in range(7)]
    return (y_prompt, y_sample, *outs_p, *outs_s)
```

```python
import functools

import jax
import jax.numpy as jnp
from jax import lax
from jax.experimental import pallas as pl
from jax.experimental.pallas import tpu as pltpu

F32 = jnp.float32
BF16 = jnp.bfloat16

D_MODEL = 2048
SEQ = 8192
DEPTH = 2
DEC_BATCH = 32
DEC_SEQ = 8
PAST_LEN = 8192
PAGE_SIZE = 128
HEAD_DIM = 64
ATTN_W = D_MODEL // 2
C_CONV = D_MODEL - ATTN_W
N_HEADS = ATTN_W // HEAD_DIM
KV_HEADS = N_HEADS // 4
GQA = N_HEADS // KV_HEADS
KV_W = KV_HEADS * HEAD_DIM
ROT_DIM = HEAD_DIM // 4
ROPE_THETA = 500000.0
CONV_WIDTH = 31
CMP_LEN = 32
CMP_STRIDE = 16
SEL_BLOCK = 64
SEL_TOPK = 16
WINDOW = 512
Q_BLOCK = 128
PLE_DIM = 256
EPS = 1e-6
NEG = -1e30
BIG = 1e9

N_SAMPLE = DEC_BATCH * DEC_SEQ
N_PAGES = PAST_LEN // PAGE_SIZE
N_CHUNK = PAST_LEN // CMP_STRIDE
N_CMP = N_CHUNK
PAGE_CHUNKS = PAGE_SIZE // CMP_STRIDE
NSB_P = SEQ // SEL_BLOCK
NSB_S = -(-(PAST_LEN + DEC_SEQ) // SEL_BLOCK)
NSB_S_PAD = 256
BELOW_NEG = -3e38
LOG2E = 1.4426950408889634
SCALE = HEAD_DIM ** -0.5 * LOG2E

LANES = 128
SUBLANES = 8
VMEM_LIMIT = 56 * 1024 * 1024

A_COLS = 4 * C_CONV
B_COLS = ATTN_W + 6 * KV_W
B_ROPE_COLS = ATTN_W + 3 * KV_W
B_KC, B_KS, B_KW, B_VC, B_VS, B_VW = (ATTN_W // KV_W + n for n in range(6))
PROJ_TM = 1024
PROJ_TN = 512
SLC_KT = 1024


def _dot(a, b):
    return jnp.dot(a, b, preferred_element_type=F32)


def _dot_nt(a, b):
    return lax.dot_general(a, b, (((1,), (1,)), ((), ())), preferred_element_type=F32)


def _lane_tile(x, n):
    return x if n == 1 else jnp.concatenate([x] * n, axis=1)


def _sigmoid(x):
    return 1.0 / (1.0 + jnp.exp(-x))


def _inproj_kernel(*refs, rope_chunks, extras):
    if extras:
        x_ref, g_ref, w_ref, wg_ref, cos_ref, sa_ref, sb_ref, u_ref, ub_ref, gate_ref, hn_ref = refs
    else:
        x_ref, g_ref, w_ref, u_ref, hn_ref = refs
    j = pl.program_id(1)

    @pl.when(j == 0)
    def _():
        x = x_ref[...]
        r = lax.rsqrt(jnp.mean(x * x, axis=-1, keepdims=True) + EPS)
        hn = ((x * r) * g_ref[...]).astype(BF16)
        hn_ref[...] = hn
        if extras:
            gate_ref[...] = _dot(hn, wg_ref[...])

    acc = _dot(hn_ref[...], w_ref[...])
    if not extras:
        u_ref[...] = acc
        return

    def store(n_rope):
        c, sa, sb = cos_ref[...], sa_ref[...], sb_ref[...]
        for k in range(acc.shape[1] // LANES):
            cs = slice(k * LANES, (k + 1) * LANES)
            y = acc[:, cs]
            if k < n_rope:
                y = y * c + pltpu.roll(y, LANES - ROT_DIM // 2, 1) * sa + pltpu.roll(y, ROT_DIM // 2, 1) * sb
            u_ref[:, cs] = y
            ub_ref[:, cs] = y.astype(BF16)

    full_tiles, part_chunks = rope_chunks // (acc.shape[1] // LANES), rope_chunks % (acc.shape[1] // LANES)
    pl.when(j < full_tiles)(lambda: store(acc.shape[1] // LANES))
    if part_chunks:
        pl.when(j == full_tiles)(lambda: store(part_chunks))
    pl.when(j >= full_tiles + bool(part_chunks))(lambda: store(0))


def _inproj(x, g, w, extras=None):
    n_rows, n_cols = x.shape[0], w.shape[1]
    tm, tn = min(PROJ_TM, n_rows), PROJ_TN
    grid = (n_rows // tm, n_cols // tn)
    row = lambda i, j: (i, 0)
    in_specs = [pl.BlockSpec((tm, D_MODEL), row),
                pl.BlockSpec((1, D_MODEL), lambda i, j: (0, 0)),
                pl.BlockSpec((D_MODEL, tn), lambda i, j: (0, j))]
    out_specs = [pl.BlockSpec((tm, tn), lambda i, j: (i, j))]
    out_shape = [jax.ShapeDtypeStruct((n_rows, n_cols), F32)]
    args = [x, g, w]
    if extras:
        wg, cos, sa, sb = extras
        in_specs += [pl.BlockSpec((D_MODEL, LANES), lambda i, j: (0, 0))] + [pl.BlockSpec((tm, LANES), row)] * 3
        out_specs += [pl.BlockSpec((tm, tn), lambda i, j: (i, j)), pl.BlockSpec((tm, LANES), row)]
        out_shape += [jax.ShapeDtypeStruct((n_rows, n_cols), BF16), jax.ShapeDtypeStruct((n_rows, LANES), F32)]
        args += [wg, cos, sa, sb]
    return pl.pallas_call(
        functools.partial(_inproj_kernel, rope_chunks=B_ROPE_COLS // LANES, extras=bool(extras)),
        grid=grid, in_specs=in_specs, out_specs=out_specs, out_shape=out_shape,
        scratch_shapes=[pltpu.VMEM((tm, D_MODEL), BF16)],
        compiler_params=pltpu.CompilerParams(dimension_semantics=("parallel", "arbitrary"),
                                             vmem_limit_bytes=VMEM_LIMIT),
        name="inproj_b" if extras else "inproj_a",
    )(*args)


HIST_PAD = 32
HIST_OFF = HIST_PAD - (CONV_WIDTH - 1)
CONV_ROWS = 128
NORM_ROWS = 32


def _conv_kernel(a_ref, b_ref, zc_ref, hist_ref, w_ref, cb_ref, lg_ref, lb_ref, yc_ref, nc_ref,
                 ext_ref, sh_ref, y_ref, *, tm):
    t = pl.program_id(1)

    @pl.when(t == 0)
    def _():
        ext_ref[0:HIST_PAD, :] = hist_ref[0]

    ext_ref[HIST_PAD:HIST_PAD + tm, :] = a_ref[...] * _sigmoid(b_ref[...])
    n_sh = tm + HIST_PAD - SUBLANES
    for s in range(1, SUBLANES):
        sh_ref[s - 1] = ext_ref[s:s + n_sh, :]
    rb = min(tm, CONV_ROWS)
    for c in range(C_CONV // LANES):
        cs = slice(c * LANES, (c + 1) * LANES)
        for r0 in range(0, tm, rb):
            acc = jnp.zeros((rb, LANES), F32) + cb_ref[:, cs]
            for k in range(CONV_WIDTH):
                base, s = (HIST_OFF + k) // SUBLANES * SUBLANES + r0, (HIST_OFF + k) % SUBLANES
                rows = ext_ref[base:base + rb, cs] if s == 0 else sh_ref[s - 1, base:base + rb, cs]
                acc = acc + rows * w_ref[k:k + 1, cs]
            y_ref[r0:r0 + rb, cs] = acc
    nb = min(tm, NORM_ROWS)
    for r0 in range(0, tm, nb):
        rs = slice(r0, r0 + nb)
        y = y_ref[rs, :]
        mu = jnp.mean(y, axis=-1, keepdims=True)
        yc = y - mu
        yn = yc * lax.rsqrt(jnp.mean(yc * yc, axis=-1, keepdims=True) + EPS) * lg_ref[...] + lb_ref[...]
        zc = zc_ref[rs, :]
        yc_ref[rs, :] = ((yn * _sigmoid(yn)) * (zc * _sigmoid(zc))).astype(yc_ref.dtype)
    tail = ext_ref[tm:tm + HIST_PAD, :]
    ext_ref[0:HIST_PAD, :] = tail

    @pl.when(t == pl.num_programs(1) - 1)
    def _():
        nc_ref[0] = tail


def _conv(ua, hist, w, cb, lg, lb, *, batch, t_len, tm):
    nt = t_len // tm
    col = lambda c: (lambda b, t: (b * nt + t, c))
    vec = pl.BlockSpec((1, C_CONV), lambda b, t: (0, 0))
    return pl.pallas_call(
        functools.partial(_conv_kernel, tm=tm),
        grid=(batch, nt),
        in_specs=[pl.BlockSpec((tm, C_CONV), col(0)), pl.BlockSpec((tm, C_CONV), col(1)),
                  pl.BlockSpec((tm, C_CONV), col(2)),
                  pl.BlockSpec((1, HIST_PAD, C_CONV), lambda b, t: (b, 0, 0)),
                  pl.BlockSpec((CONV_WIDTH, C_CONV), lambda b, t: (0, 0)), vec, vec, vec],
        out_specs=[pl.BlockSpec((tm, C_CONV), lambda b, t: (b * nt + t, 0)),
                   pl.BlockSpec((1, HIST_PAD, C_CONV), lambda b, t: (b, 0, 0))],
        out_shape=[jax.ShapeDtypeStruct((batch * t_len, C_CONV), BF16 if tm % 16 == 0 else F32),
                   jax.ShapeDtypeStruct((batch, HIST_PAD, C_CONV), F32)],
        scratch_shapes=[pltpu.VMEM((HIST_PAD + tm, C_CONV), F32),
                        pltpu.VMEM((SUBLANES - 1, tm + HIST_PAD - SUBLANES, C_CONV), F32),
                        pltpu.VMEM((tm, C_CONV), F32)],
        compiler_params=pltpu.CompilerParams(dimension_semantics=("parallel", "arbitrary"),
                                             vmem_limit_bytes=VMEM_LIMIT),
        name="conv_group",
    )(ua, ua, ua, hist, w, cb, lg, lb)


CMP_ROWS = PAST_LEN + CMP_STRIDE


LANE_HALVES = KV_W // LANES


def _compress_rows(x_ref, w_ref):
    acc = None
    for l in range(CMP_LEN):
        xl = jnp.concatenate([x_ref[c, pl.ds(l, N_CMP, stride=CMP_STRIDE), :] for c in range(LANE_HALVES)], axis=1)
        part = _dot(xl.astype(BF16), w_ref[l])
        acc = part if acc is None else acc + part
    return acc


def _pcompress_kernel(x_ref, w_ref, o_ref, rows_ref):
    for c in range(LANE_HALVES):
        rows_ref[c, 0:SEQ, :] = x_ref[:, c * LANES:(c + 1) * LANES]
        rows_ref[c, SEQ:, :] = jnp.zeros((CMP_ROWS - SEQ, LANES), F32)
    o_ref[...] = _compress_rows(rows_ref, w_ref).astype(o_ref.dtype)


def _pcompress(ub, col_block, w):
    once = pl.Buffered(1)
    return pl.pallas_call(
        _pcompress_kernel,
        grid=(1,),
        in_specs=[pl.BlockSpec((SEQ, KV_W), lambda i: (0, col_block), pipeline_mode=once),
                  pl.BlockSpec((CMP_LEN, KV_W, KV_W), lambda i: (0, 0, 0), pipeline_mode=once)],
        out_specs=pl.BlockSpec((N_CMP, KV_W), lambda i: (0, 0)),
        out_shape=jax.ShapeDtypeStruct((N_CMP, KV_W), BF16),
        scratch_shapes=[pltpu.VMEM((LANE_HALVES, CMP_ROWS, LANES), F32)],
        compiler_params=pltpu.CompilerParams(dimension_semantics=("arbitrary",), vmem_limit_bytes=VMEM_LIMIT),
        name="compress_prompt",
    )(ub, w)


S_CHUNKS = N_CHUNK + SUBLANES


def _scompress_kernel(pt_ref, cache_ref, new_ref, perm_ref, wab_ref, o_ref, tbuf, xc_ref, sem, *, layer):
    b = pl.program_id(0)
    slot = b % 2

    def copies(bb, sl, go):
        def body(p, carry):
            cp = pltpu.make_async_copy(cache_ref.at[layer, pt_ref[bb, p]],
                                       tbuf.at[sl, :, pl.ds(pl.multiple_of(p * PAGE_SIZE, PAGE_SIZE), PAGE_SIZE)],
                                       sem.at[sl])
            cp.start() if go else cp.wait()
            return carry
        lax.fori_loop(0, N_PAGES, body, 0)

    @pl.when(b == 0)
    def _():
        copies(0, 0, True)

    @pl.when(b + 1 < pl.num_programs(0))
    def _():
        copies(b + 1, 1 - slot, True)

    copies(b, slot, False)
    perm = perm_ref[...]
    for pp in range(N_PAGES // 2):
        pages = tbuf[slot, :, pp * 2 * PAGE_SIZE:(pp + 1) * 2 * PAGE_SIZE].T.astype(BF16)
        rows = _dot(perm, pages)
        for k in range(2):
            p = 2 * pp + k
            for l in range(CMP_STRIDE):
                r0 = k * PAGE_SIZE + l * PAGE_CHUNKS
                for c in range(LANE_HALVES):
                    xc_ref[c, p * PAGE_CHUNKS:(p + 1) * PAGE_CHUNKS, l * LANES:(l + 1) * LANES] = (
                        rows[r0:r0 + PAGE_CHUNKS, c * LANES:(c + 1) * LANES])
    for c in range(LANE_HALVES):
        cs = slice(c * LANES, (c + 1) * LANES)
        xc_ref[c, N_CHUNK:, :] = jnp.zeros((SUBLANES, CMP_STRIDE * LANES), F32)
        for l in range(DEC_SEQ):
            xc_ref[c, N_CHUNK:N_CHUNK + 1, l * LANES:(l + 1) * LANES] = new_ref[0, l:l + 1, cs]
        z = _dot(xc_ref[c].astype(BF16), wab_ref[c])
        zb = pltpu.roll(z[:, LANES:], S_CHUNKS - 1, 0)
        o_ref[0, :, cs] = (z[0:N_CMP, :LANES] + zb[0:N_CMP]).astype(o_ref.dtype)


def _scompress(page_table, cache_t, new_rows, perm, wab, layer):
    once = pl.Buffered(1)
    return pl.pallas_call(
        functools.partial(_scompress_kernel, layer=layer),
        grid_spec=pltpu.PrefetchScalarGridSpec(
            num_scalar_prefetch=1, grid=(DEC_BATCH,),
            in_specs=[pl.BlockSpec(memory_space=pl.ANY),
                      pl.BlockSpec((1, DEC_SEQ, KV_W), lambda b, pt: (b, 0, 0)),
                      pl.BlockSpec((2 * PAGE_SIZE, 2 * PAGE_SIZE), lambda b, pt: (0, 0), pipeline_mode=once),
                      pl.BlockSpec((LANE_HALVES, CMP_STRIDE * LANES, 2 * LANES), lambda b, pt: (0, 0, 0),
                                   pipeline_mode=once)],
            out_specs=pl.BlockSpec((1, N_CMP, KV_W), lambda b, pt: (b, 0, 0)),
            scratch_shapes=[pltpu.VMEM((2, KV_W, PAST_LEN), F32),
                            pltpu.VMEM((LANE_HALVES, S_CHUNKS, CMP_STRIDE * LANES), F32),
                            pltpu.SemaphoreType.DMA((2,))]),
        out_shape=jax.ShapeDtypeStruct((DEC_BATCH, N_CMP, KV_W), BF16),
        compiler_params=pltpu.CompilerParams(dimension_semantics=("arbitrary",), vmem_limit_bytes=VMEM_LIMIT),
        name="compress_sample",
    )(page_table, cache_t, new_rows, perm, wab)


def _cmp_branch(q, tcol, ck, cv):
    s = _dot_nt(q, ck)
    n = lax.broadcasted_iota(jnp.int32, s.shape, 1)
    vis = (n * CMP_STRIDE + (CMP_LEN - 1)) <= tcol
    sm = jnp.where(vis, s, NEG)
    e = jnp.where(vis, jnp.exp2(sm - jnp.max(sm, axis=1, keepdims=True)), 0.0)
    den = jnp.sum(e, axis=1, keepdims=True)
    p = e / jnp.where(den > 0.0, den, 1.0)
    return _dot(p.astype(BF16), cv), p


def _select_bias(imp, t, nsb, axis):
    j = lax.broadcasted_iota(jnp.int32, imp.shape, axis)
    cur = lax.shift_right_arithmetic(t, 6)
    forced = (j == 0) | (j == cur) | (j == cur - 1)
    score = jnp.where(j * SEL_BLOCK <= t, jnp.where(forced, BIG, imp), NEG)
    score = jnp.where(j < nsb, score, BELOW_NEG)
    jf = j.astype(F32)
    bias = jnp.full(imp.shape, NEG, F32)
    for _ in range(SEL_TOPK):
        mx = jnp.max(score, axis=axis, keepdims=True)
        idx = jnp.min(jnp.where(score == mx, jf, 1e9), axis=axis, keepdims=True)
        pick = jf == idx
        bias = jnp.where(pick, 0.0, bias)
        score = jnp.where(pick, BELOW_NEG, score)
    return bias


def _block_onehot(nb, kt, first_block):
    j = lax.broadcasted_iota(jnp.int32, (nb, kt), 0)
    col = lax.broadcasted_iota(jnp.int32, (nb, kt), 1)
    return jnp.where(j - lax.shift_right_arithmetic(col, 6) == first_block, 1.0, 0.0).astype(BF16)


def _online_update(s, v, m_ref, l_ref, acc_ref, rows, v_transposed=False):
    m_prev, l_prev = m_ref[rows, :], l_ref[rows, :]
    m_next = jnp.maximum(m_prev, jnp.max(s, axis=1, keepdims=True))
    alpha = jnp.exp2(m_prev - m_next)
    p = jnp.exp2(s - _lane_tile(m_next, s.shape[1] // LANES))
    l_ref[rows, :] = alpha * l_prev + jnp.sum(p, axis=1, keepdims=True)
    m_ref[rows, :] = m_next
    pv = _dot_nt(p.astype(BF16), v) if v_transposed else _dot(p.astype(BF16), v)
    acc_ref[rows, :] = acc_ref[rows, :] * _lane_tile(alpha, KV_W // LANES) + pv


def _window_mask(tcol, kpos):
    return lax.bitcast_convert_type(tcol - kpos, jnp.uint32) <= jnp.uint32(WINDOW)


GROUP_ROWS = GQA * Q_BLOCK


PACK_TM = 512


def _pack_keys_kernel(k_ref, o_ref):
    i = pl.program_id(0)
    lane = lax.broadcasted_iota(jnp.int32, (PACK_TM, LANES), 1)
    blk = lax.shift_right_arithmetic(i * PACK_TM + lax.broadcasted_iota(jnp.int32, (PACK_TM, LANES), 0), 6)
    lo = lane < HEAD_DIM
    hot_lo = jnp.where(lane - HEAD_DIM == blk, 1.0, 0.0)
    hot_hi = jnp.where(lane + HEAD_DIM == blk, 1.0, 0.0)
    for g in range(KV_HEADS):
        pair = k_ref[:, (g // 2) * LANES:(g // 2 + 1) * LANES]
        if g % 2:
            pair = pltpu.roll(pair, HEAD_DIM, 1)
        o_ref[g, :, 0:LANES] = jnp.where(lo, pair, hot_lo).astype(BF16)
        o_ref[g, :, LANES:2 * LANES] = jnp.where(lo, hot_hi, 0.0).astype(BF16)


def _pack_keys(ub, col_block):
    assert NSB_P == 2 * HEAD_DIM and KV_W == 2 * LANES
    return pl.pallas_call(
        _pack_keys_kernel,
        grid=(SEQ // PACK_TM,),
        in_specs=[pl.BlockSpec((PACK_TM, KV_W), lambda i: (i, col_block))],
        out_specs=pl.BlockSpec((KV_HEADS, PACK_TM, KV_W), lambda i: (0, i, 0)),
        out_shape=jax.ShapeDtypeStruct((KV_HEADS, SEQ, KV_W), BF16),
        compiler_params=pltpu.CompilerParams(dimension_semantics=("parallel",), vmem_limit_bytes=VMEM_LIMIT),
        name="pack_keys",
    )(ub)


def _pattn_kernel(q_ref, ck_ref, cv_ref, mm_ref, ka_ref, vs_ref, kw_ref, vw_ref, gt_ref, za_ref, o_ref,
                  qbd_ref, qa_ref, oc_ref, acc_ref, m_ref, l_ref):
    i = pl.program_id(0)
    q0 = i * Q_BLOCK
    T, R = Q_BLOCK, GROUP_ROWS
    lane = lax.broadcasted_iota(jnp.int32, (T, LANES), 1)
    half = lax.shift_right_arithmetic(lane, 6)
    lo = lane < HEAD_DIM
    tcol = q0 + (lax.broadcasted_iota(jnp.int32, (R, 1), 0) & (T - 1))
    trow = q0 + lax.broadcasted_iota(jnp.int32, (1, T), 1)

    zeros = jnp.zeros((T, LANES), BF16)
    for hh in range(N_HEADS):
        g = hh // GQA
        pair = q_ref[:, (hh // 2) * LANES:(hh // 2 + 1) * LANES] * SCALE
        if hh % 2 != g % 2:
            pair = pltpu.roll(pair, HEAD_DIM, 1)
        x = jnp.where(half == g % 2, pair, 0.0).astype(BF16)
        qbd_ref[hh * T:(hh + 1) * T, 0:LANES] = x if g // 2 == 0 else zeros
        qbd_ref[hh * T:(hh + 1) * T, LANES:2 * LANES] = zeros if g // 2 == 0 else x

    for g in range(KV_HEADS):
        rows = slice(g * R, (g + 1) * R)
        o_c, p = _cmp_branch(qbd_ref[rows, :], tcol, ck_ref[...], cv_ref[...])
        oc_ref[rows, :] = o_c
        imp_rows = _dot(p.astype(BF16), mm_ref[...])
        imp = imp_rows[0:T] + imp_rows[T:2 * T] + imp_rows[2 * T:3 * T] + imp_rows[3 * T:4 * T]
        bias = _select_bias(imp.T, trow, NSB_P, axis=0).T
        bias = pltpu.roll(bias, HEAD_DIM, 1)
        for h in range(GQA):
            hh = g * GQA + h
            pair = q_ref[:, (hh // 2) * LANES:(hh // 2 + 1) * LANES] * SCALE
            if hh % 2:
                pair = pltpu.roll(pair, HEAD_DIM, 1)
            qa_ref[hh * T:(hh + 1) * T, 0:LANES] = jnp.where(lo, pair, bias).astype(BF16)
            qa_ref[hh * T:(hh + 1) * T, LANES:2 * LANES] = jnp.where(lo, bias, 0.0).astype(BF16)

    m_ref[...] = jnp.full(m_ref.shape, NEG, F32)
    l_ref[...] = jnp.zeros(l_ref.shape, F32)
    acc_ref[...] = jnp.zeros(acc_ref.shape, F32)
    kcol = lax.broadcasted_iota(jnp.int32, (1, SLC_KT), 1)

    def slc_tile(c, causal):
        k0 = pl.multiple_of(c * SLC_KT, SLC_KT)
        v_t = vs_ref[pl.ds(k0, SLC_KT), :]
        for g in range(KV_HEADS):
            rows = slice(g * R, (g + 1) * R)
            s = _dot_nt(qa_ref[rows, :], ka_ref[g, pl.ds(k0, SLC_KT), :])
            if causal:
                s = jnp.where(k0 + kcol <= tcol, s, NEG)
            _online_update(s, v_t, m_ref, l_ref, acc_ref, rows)

    c_last = (q0 + T - 1) // SLC_KT

    def full_tile(c, carry):
        slc_tile(c, False)
        return carry

    lax.fori_loop(0, c_last, full_tile, 0)
    slc_tile(c_last, True)

    start = pl.multiple_of(jnp.maximum(q0 - WINDOW, 0), Q_BLOCK)
    kw_t = kw_ref[pl.ds(start, WINDOW + T), :]
    vw_t = vw_ref[pl.ds(start, WINDOW + T), :]
    kpos = start + lax.broadcasted_iota(jnp.int32, (1, WINDOW + T), 1)
    sig = _sigmoid(gt_ref[...])
    for g in range(KV_HEADS):
        rows = slice(g * R, (g + 1) * R)
        cc = slice((g // 2) * LANES, (g // 2 + 1) * LANES)
        s3 = jnp.where(_window_mask(tcol, kpos), _dot_nt(qbd_ref[rows, :], kw_t), NEG)
        e3 = jnp.exp2(s3 - jnp.max(s3, axis=1, keepdims=True))
        o_w = _dot(e3.astype(BF16), vw_t)[:, cc] / jnp.sum(e3, axis=1, keepdims=True)
        o_s = acc_ref[rows, cc] / l_ref[rows, :]
        o_c = oc_ref[rows, cc]
        for pair in range(2):
            pieces = []
            for sub in range(2):
                h = 2 * pair + sub
                hh = g * GQA + h
                hr = slice(h * T, (h + 1) * T)
                comb = (sig[:, hh:hh + 1] * o_c[hr] + sig[:, N_HEADS + hh:N_HEADS + hh + 1] * o_s[hr]
                        + sig[:, 2 * N_HEADS + hh:2 * N_HEADS + hh + 1] * o_w[hr])
                pieces.append(comb if g % 2 == sub else pltpu.roll(comb, HEAD_DIM, 1))
            oc_cols = slice((2 * g + pair) * LANES, (2 * g + pair + 1) * LANES)
            za = za_ref[:, oc_cols]
            o_ref[:, oc_cols] = (jnp.where(half == 0, pieces[0], pieces[1]) * (za * _sigmoid(za))).astype(BF16)


def _pattn(ub, ck, cv, mmat, kaug, ubb, gates, ua):
    const = lambda r, c: (lambda i: (r, c))
    once = pl.Buffered(1)
    resident = lambda c: pl.BlockSpec((SEQ, KV_W), const(0, c), pipeline_mode=once)
    small = lambda shape: pl.BlockSpec(shape, const(0, 0))
    return pl.pallas_call(
        _pattn_kernel,
        grid=(SEQ // Q_BLOCK,),
        in_specs=[pl.BlockSpec((Q_BLOCK, ATTN_W), lambda i: (i, 0)),
                  small((N_CMP, KV_W)), small((N_CMP, KV_W)), small((N_CMP, NSB_P)),
                  pl.BlockSpec((KV_HEADS, SEQ, KV_W), lambda i: (0, 0, 0), pipeline_mode=once),
                  resident(B_VS), resident(B_KW), resident(B_VW),
                  pl.BlockSpec((Q_BLOCK, LANES), lambda i: (i, 0)),
                  pl.BlockSpec((Q_BLOCK, ATTN_W), lambda i: (i, 3))],
        out_specs=pl.BlockSpec((Q_BLOCK, ATTN_W), lambda i: (i, 0)),
        out_shape=jax.ShapeDtypeStruct((SEQ, ATTN_W), BF16),
        scratch_shapes=[pltpu.VMEM((N_HEADS * Q_BLOCK, KV_W), BF16),
                        pltpu.VMEM((N_HEADS * Q_BLOCK, KV_W), BF16),
                        pltpu.VMEM((N_HEADS * Q_BLOCK, KV_W), F32),
                        pltpu.VMEM((N_HEADS * Q_BLOCK, KV_W), F32),
                        pltpu.VMEM((N_HEADS * Q_BLOCK, LANES), F32),
                        pltpu.VMEM((N_HEADS * Q_BLOCK, LANES), F32)],
        compiler_params=pltpu.CompilerParams(dimension_semantics=("parallel",), vmem_limit_bytes=VMEM_LIMIT),
        name="attn_prompt",
    )(ub, ck, cv, mmat, kaug, ubb, ubb, ubb, gates, ua)


SLC_PAGES = 16
SLC_CHUNKS = N_PAGES // SLC_PAGES
SLC_CHUNK_ROWS = SLC_PAGES * PAGE_SIZE
S_ROWS = N_HEADS * DEC_SEQ
NEW_PAD = LANES


def _sattn_kernel(pt_ref, qt_ref, ck_ref, cv_ref, mm_ref, ksc_ref, vsc_ref, ksn_ref, vsn_ref,
                  kwc_ref, vwc_ref, kwn_ref, vwn_ref, gt_ref, za_ref, o_ref,
                  kbuf, vbuf, ksem, vsem, newk_ref, newv_ref, s_ref, *, layer):
    b = pl.program_id(0)
    nb = pl.num_programs(0)
    R = S_ROWS
    row = lax.broadcasted_iota(jnp.int32, (R, 1), 0)
    tcol = PAST_LEN + (row & (DEC_SEQ - 1))
    lane_group = lax.shift_right_arithmetic(lax.broadcasted_iota(jnp.int32, (R, KV_W), 1), 6)
    diag = lane_group == lax.shift_right_arithmetic(row, 5)
    vslot = b % 2

    def page_copies(cache, buf, sem, bb, first_page, n_pages, slot, go):
        def body(p, carry):
            cp = pltpu.make_async_copy(cache.at[layer, pt_ref[bb, first_page + p]],
                                       buf.at[slot, :, pl.ds(pl.multiple_of(p * PAGE_SIZE, PAGE_SIZE), PAGE_SIZE)],
                                       sem.at[slot])
            cp.start() if go else cp.wait()
            return carry
        lax.fori_loop(0, n_pages, body, 0)

    def key_pages(bb, slot, go):
        page_copies(ksc_ref, kbuf, ksem, bb, 0, N_PAGES, slot, go)

    def value_pages(bb, slot, go):
        page_copies(vsc_ref, vbuf, vsem, bb, 0, N_PAGES, slot, go)

    @pl.when(b == 0)
    def _():
        key_pages(0, 0, True)
        value_pages(0, 0, True)

    @pl.when(b + 1 < nb)
    def _():
        key_pages(b + 1, 1 - vslot, True)
        value_pages(b + 1, 1 - vslot, True)

    q = jnp.where(diag, qt_ref[0] * SCALE, 0.0).astype(BF16)

    o_c, p = _cmp_branch(q, tcol, ck_ref[0], cv_ref[0])
    T = DEC_SEQ
    imp_rows = _dot(p.astype(BF16), mm_ref[...])
    groups = []
    for r0 in range(0, R, GQA * T):
        grp = imp_rows[r0:r0 + T] + imp_rows[r0 + T:r0 + 2 * T] + imp_rows[r0 + 2 * T:r0 + 3 * T] + imp_rows[r0 + 3 * T:r0 + 4 * T]
        groups += [grp] * GQA
    imp = jnp.concatenate(groups, axis=0)
    trow = PAST_LEN + (lax.broadcasted_iota(jnp.int32, (1, R), 1) & (T - 1))
    a = _select_bias(imp.T, trow, NSB_S, axis=0).T.astype(BF16)

    def lane_blocks(x):
        return [x[:, j * LANES:(j + 1) * LANES] for j in range(x.shape[1] // LANES)]

    m_lanes = jnp.full((R, LANES), NEG, F32)
    key_pages(b, vslot, False)
    for c in range(SLC_CHUNKS):
        cs = slice(c * SLC_CHUNK_ROWS, (c + 1) * SLC_CHUNK_ROWS)
        onehot = _block_onehot(NSB_S_PAD, SLC_CHUNK_ROWS, c * SLC_CHUNK_ROWS // SEL_BLOCK)
        s = _dot(q, kbuf[vslot, :, cs].astype(BF16)) + _dot(a, onehot)
        s_ref[:, cs] = s
        for blk in lane_blocks(s):
            m_lanes = jnp.maximum(m_lanes, blk)

    newcol = lax.broadcasted_iota(jnp.int32, (1, NEW_PAD), 1)
    new_ok = PAST_LEN + newcol <= tcol
    newk_ref[...] = jnp.zeros(newk_ref.shape, BF16)
    newv_ref[...] = jnp.zeros(newv_ref.shape, BF16)
    newk_ref[0:T, :] = ksn_ref[0].astype(BF16)
    newv_ref[0:T, :] = vsn_ref[0].astype(BF16)
    onehot = _block_onehot(NSB_S_PAD, NEW_PAD, PAST_LEN // SEL_BLOCK)
    s_new = jnp.where(new_ok, _dot_nt(q, newk_ref[...]) + _dot(a, onehot), NEG)
    m = jnp.max(jnp.maximum(m_lanes, s_new), axis=1, keepdims=True)

    value_pages(b, vslot, False)
    p_new = jnp.exp2(s_new - m)
    l_lanes = p_new
    acc = _dot(p_new.astype(BF16), newv_ref[...])
    for c in range(SLC_CHUNKS):
        cs = slice(c * SLC_CHUNK_ROWS, (c + 1) * SLC_CHUNK_ROWS)
        p = jnp.exp2(s_ref[:, cs] - m)
        for blk in lane_blocks(p):
            l_lanes = l_lanes + blk
        acc = acc + _dot_nt(p.astype(BF16), vbuf[vslot, :, cs].astype(BF16))
    o_s = acc / jnp.sum(l_lanes, axis=1, keepdims=True)

    wpos = (PAST_LEN - WINDOW) + lax.broadcasted_iota(jnp.int32, (1, WINDOW), 1)
    newk_ref[0:T, :] = kwn_ref[0].astype(BF16)
    newv_ref[0:T, :] = vwn_ref[0].astype(BF16)
    s3 = jnp.concatenate(
        [jnp.where(_window_mask(tcol, wpos), _dot(q, kwc_ref[0, 0].astype(BF16)), NEG),
         jnp.where(new_ok, _dot_nt(q, newk_ref[...]), NEG)], axis=1)
    e3 = jnp.exp2(s3 - jnp.max(s3, axis=1, keepdims=True))
    p3 = e3.astype(BF16)
    o_w = (_dot_nt(p3[:, 0:WINDOW], vwc_ref[0, 0].astype(BF16)) + _dot(p3[:, WINDOW:], newv_ref[...]))
    o_w = o_w / jnp.sum(e3, axis=1, keepdims=True)

    sig = _sigmoid(gt_ref[0])
    comb = jnp.where(diag, sig[:, 0:1] * o_c + sig[:, 1:2] * o_s + sig[:, 2:3] * o_w, 0.0)
    folded = comb[:, 0:LANES] + comb[:, LANES:2 * LANES]
    folded = folded + pltpu.roll(folded, HEAD_DIM, 1)
    za = za_ref[0]
    o_ref[0] = folded[:, 0:HEAD_DIM] * (za * _sigmoid(za))


def _sattn(page_table, qt, ck, cv, mmat, cache_k, cache_v, ksn, vsn, win_k, win_v, kwn, vwn, gates, za, layer):
    per_b = lambda shape: pl.BlockSpec((1,) + shape, lambda b, pt: (b,) + (0,) * len(shape))
    win = pl.BlockSpec((1, 1, KV_W, WINDOW), lambda b, pt: (layer, b, 0, 0))
    hbm = pl.BlockSpec(memory_space=pl.ANY)
    return pl.pallas_call(
        functools.partial(_sattn_kernel, layer=layer),
        grid_spec=pltpu.PrefetchScalarGridSpec(
            num_scalar_prefetch=1, grid=(DEC_BATCH,),
            in_specs=[per_b((S_ROWS, KV_W)), per_b((N_CMP, KV_W)), per_b((N_CMP, KV_W)),
                      pl.BlockSpec((N_CMP, NSB_S_PAD), lambda b, pt: (0, 0)),
                      hbm, hbm, per_b((DEC_SEQ, KV_W)), per_b((DEC_SEQ, KV_W)),
                      win, win, per_b((DEC_SEQ, KV_W)), per_b((DEC_SEQ, KV_W)),
                      per_b((S_ROWS, LANES)), per_b((S_ROWS, HEAD_DIM))],
            out_specs=per_b((S_ROWS, HEAD_DIM)),
            scratch_shapes=[pltpu.VMEM((2, KV_W, PAST_LEN), F32), pltpu.VMEM((2, KV_W, PAST_LEN), F32),
                            pltpu.SemaphoreType.DMA((2,)), pltpu.SemaphoreType.DMA((2,)),
                            pltpu.VMEM((NEW_PAD, KV_W), BF16), pltpu.VMEM((NEW_PAD, KV_W), BF16),
                            pltpu.VMEM((S_ROWS, PAST_LEN), F32)]),
        out_shape=jax.ShapeDtypeStruct((DEC_BATCH, S_ROWS, HEAD_DIM), F32),
        compiler_params=pltpu.CompilerParams(dimension_semantics=("arbitrary",), vmem_limit_bytes=VMEM_LIMIT),
        name="attn_sample",
    )(page_table, qt, ck, cv, mmat, cache_k, cache_v, ksn, vsn, win_k, win_v, kwn, vwn, gates, za)


OUT_TM = 512


def _outproj_kernel(h_ref, yc_ref, ya_ref, pe_ref, wc_ref, wa_ref, wg_ref, wp_ref, fg_ref, o_ref, *, final):
    h1 = h_ref[...] + (_dot(yc_ref[...], wc_ref[...]) + _dot(ya_ref[...], wa_ref[...]))
    h2 = h1 + _sigmoid(_dot(h1.astype(BF16), wg_ref[...])) * _dot(pe_ref[...].astype(BF16), wp_ref[...])
    if final:
        h2 = (h2 * lax.rsqrt(jnp.mean(h2 * h2, axis=-1, keepdims=True) + EPS)) * fg_ref[...]
    o_ref[...] = h2


def _outproj(h, yc, ya, pe, wc, wa, wg, wp, fg, final):
    n_rows = h.shape[0]
    tm = min(OUT_TM, n_rows)
    row = lambda w: pl.BlockSpec((tm, w), lambda i: (i, 0))
    const = lambda shape: pl.BlockSpec(shape, lambda i: (0, 0), pipeline_mode=pl.Buffered(1))
    return pl.pallas_call(
        functools.partial(_outproj_kernel, final=final),
        grid=(n_rows // tm,),
        in_specs=[row(D_MODEL), row(C_CONV), row(ATTN_W), row(PLE_DIM),
                  const((C_CONV, D_MODEL)), const((ATTN_W, D_MODEL)), const((D_MODEL, D_MODEL)),
                  const((PLE_DIM, D_MODEL)), const((1, D_MODEL))],
        out_specs=row(D_MODEL),
        out_shape=jax.ShapeDtypeStruct((n_rows, D_MODEL), F32),
        compiler_params=pltpu.CompilerParams(dimension_semantics=("parallel",), vmem_limit_bytes=VMEM_LIMIT),
        name="outproj_final" if final else "outproj",
    )(h, yc, ya, pe, wc, wa, wg, wp, fg)


def _rope_tables(pos):
    n_rows = pos.shape[0]
    half = ROT_DIM // 2
    inv = ROPE_THETA ** (-2.0 * jnp.arange(half, dtype=F32) / ROT_DIM)
    ang = pos.astype(F32)[:, None] * inv[None, :]
    cos, sin = jnp.cos(ang), jnp.sin(ang)
    one = jnp.ones((n_rows, HEAD_DIM - ROT_DIM), F32)
    zero = lambda n: jnp.zeros((n_rows, n), F32)
    c = jnp.concatenate([cos, cos, one], axis=1)
    sa = jnp.concatenate([-sin, zero(HEAD_DIM - half)], axis=1)
    sb = jnp.concatenate([zero(half), sin, zero(HEAD_DIM - ROT_DIM)], axis=1)
    return tuple(jnp.concatenate([t, t], axis=1) for t in (c, sa, sb))


def _cmp_to_sel(nsb, pad):
    n = jnp.arange(N_CMP)[:, None] * CMP_STRIDE
    j = jnp.arange(pad)[None, :]
    m = (n < (j + 1) * SEL_BLOCK) & (n + CMP_LEN > j * SEL_BLOCK) & (j < nsb)
    return m.astype(BF16)


def _expand_cmp_weight(w):
    w3 = w.reshape(CMP_LEN, HEAD_DIM, HEAD_DIM)
    eye = jnp.eye(KV_HEADS, dtype=w.dtype)
    return jnp.einsum('lde,gh->lgdhe', w3, eye).reshape(CMP_LEN, KV_W, KV_W).astype(BF16)


def _pages_transposed(cache):
    d, n, p = cache.shape[:3]
    return cache.transpose(0, 1, 3, 4, 2).reshape(d, n, KV_W, p)


def _to_head_rows(x, width):
    return x.reshape(DEC_BATCH, DEC_SEQ, N_HEADS, width).transpose(0, 2, 1, 3).reshape(DEC_BATCH, S_ROWS, width)


def kernel(x_prompt, x_sample, cache_cmp_k, cache_cmp_v, cache_slc_k, cache_slc_v, cache_win_k, cache_win_v,
           state_conv, page_table, p_prompt, p_sample, norm_g, w_in, conv_w, conv_b, conv_ln_g, conv_ln_b,
           w_cmp_k, w_cmp_v, w_out, w_ple, w_ple_gate, final_g):
    h_p, h_s = x_prompt.reshape(SEQ, D_MODEL), x_sample.reshape(N_SAMPLE, D_MODEL)
    rope_p = _rope_tables(jnp.arange(SEQ))
    rope_s = _rope_tables(PAST_LEN + jnp.tile(jnp.arange(DEC_SEQ), DEC_BATCH))
    mm_p = _cmp_to_sel(NSB_P, NSB_P)
    mm_s = _cmp_to_sel(NSB_S, NSB_S_PAD)
    cck, ccv = _pages_transposed(cache_cmp_k), _pages_transposed(cache_cmp_v)
    csk, csv = _pages_transposed(cache_slc_k), _pages_transposed(cache_slc_v)
    cwk, cwv = _pages_transposed(cache_win_k), _pages_transposed(cache_win_v)
    page_row = jnp.arange(2 * PAGE_SIZE)
    in_page = page_row % PAGE_SIZE
    perm_col = (page_row - in_page) + (in_page % CMP_STRIDE) * PAGE_CHUNKS + in_page // CMP_STRIDE
    perm = (page_row[:, None] == perm_col[None, :]).astype(BF16)
    hist_p = jnp.zeros((1, HIST_PAD, C_CONV), F32)
    hist_s = jnp.pad(state_conv, ((0, 0), (0, 0), (HIST_OFF, 0), (0, 0)))

    o_q = 3 * C_CONV
    o_kv = o_q + ATTN_W
    o_gt = o_kv + 6 * KV_W
    o_za = o_gt + 3 * N_HEADS
    kv = lambda wi, n: wi[:, o_kv + n * KV_W:o_kv + (n + 1) * KV_W]

    st_p, st_s = [], []
    for i in range(DEPTH):
        wi = w_in[i]
        w_a = jnp.concatenate([wi[:, :o_q], wi[:, o_za:]], axis=1).astype(BF16)
        w_b = jnp.concatenate([wi[:, o_q:o_kv], kv(wi, 0), kv(wi, 2), kv(wi, 4), kv(wi, 1), kv(wi, 3), kv(wi, 5)],
                              axis=1).astype(BF16)
        w_g = jnp.pad(wi[:, o_gt:o_za], ((0, 0), (0, LANES - 3 * N_HEADS))).astype(BF16)
        g_row = norm_g[i].reshape(1, D_MODEL)

        conv_args = (conv_w[i], conv_b[i].reshape(1, C_CONV), conv_ln_g[i].reshape(1, C_CONV),
                     conv_ln_b[i].reshape(1, C_CONV))
        wab_k, wab_v = _expand_cmp_weight(w_cmp_k[i]), _expand_cmp_weight(w_cmp_v[i])
        wo = w_out[i].astype(BF16)
        out_w = (wo[:C_CONV], wo[C_CONV:], w_ple_gate[i].astype(BF16), w_ple[i].astype(BF16), final_g.reshape(1, D_MODEL))
        final = i == DEPTH - 1
        seg = lambda u, n: u[:, (B_KC + n) * KV_W:(B_KC + n + 1) * KV_W]
        heads = lambda x, b, t: x.reshape(b, t, KV_HEADS, HEAD_DIM)

        ua, = _inproj(h_p, g_row, w_a)
        ub, ubb, gates = _inproj(h_p, g_row, w_b, extras=(w_g, *rope_p))
        yc, nc_p = _conv(ua, hist_p, *conv_args, batch=1, t_len=SEQ, tm=256)
        ck = _pcompress(ub, B_KC, wab_k)
        cv = _pcompress(ub, B_VC, wab_v)
        ya = _pattn(ub, ck, cv, mm_p, _pack_keys(ub, B_KS), ubb, gates, ua)
        h_p = _outproj(h_p, yc, ya, p_prompt[i].reshape(SEQ, PLE_DIM), *out_w, final=final)
        kc, ks, kw, vc, vs, vw = (seg(ub, n) for n in range(6))
        wb = min(WINDOW, SEQ)
        st_p.append((heads(kc, 1, SEQ), heads(vc, 1, SEQ), heads(ks, 1, SEQ), heads(vs, 1, SEQ),
                     heads(kw[SEQ - wb:], 1, wb), heads(vw[SEQ - wb:], 1, wb), nc_p[:, HIST_OFF:]))

        ua, = _inproj(h_s, g_row, w_a)
        ub, _, gates = _inproj(h_s, g_row, w_b, extras=(w_g, *rope_s))
        yc, nc_s = _conv(ua, hist_s[i], *conv_args, batch=DEC_BATCH, t_len=DEC_SEQ, tm=DEC_SEQ)
        kc, ks, kw, vc, vs, vw = (seg(ub, n).reshape(DEC_BATCH, DEC_SEQ, KV_W) for n in range(6))
        halves = lambda w: jnp.stack([jnp.concatenate(
            [w[s:s + CMP_STRIDE, c * LANES:(c + 1) * LANES, c * LANES:(c + 1) * LANES].reshape(CMP_STRIDE * LANES, LANES)
             for s in (0, CMP_STRIDE)], axis=1) for c in range(LANE_HALVES)])
        ck = _scompress(page_table, cck, kc, perm, halves(wab_k), i)
        cv = _scompress(page_table, ccv, vc, perm, halves(wab_v), i)
        qt = jnp.tile(_to_head_rows(ub[:, :ATTN_W], HEAD_DIM), (1, 1, KV_HEADS))
        gt = gates[:, :3 * N_HEADS].reshape(DEC_BATCH, DEC_SEQ, 3, N_HEADS).transpose(0, 3, 1, 2)
        gt = jnp.pad(gt.reshape(DEC_BATCH, S_ROWS, 3), ((0, 0), (0, 0), (0, LANES - 3)))
        za = _to_head_rows(ua[:, 3 * C_CONV:], HEAD_DIM)
        ya = _sattn(page_table, qt, ck, cv, mm_s, csk, csv, ks, vs, cwk, cwv, kw, vw, gt, za, i)
        ya = ya.reshape(DEC_BATCH, N_HEADS, DEC_SEQ, HEAD_DIM).transpose(0, 2, 1, 3).reshape(N_SAMPLE, ATTN_W)
        h_s = _outproj(h_s, yc.astype(BF16), ya.astype(BF16), p_sample[i].reshape(N_SAMPLE, PLE_DIM), *out_w, final=final)
        hs = lambda x: heads(x, DEC_BATCH, DEC_SEQ)
        st_s.append((hs(kc), hs(vc), hs(ks), hs(vs),
                     jnp.concatenate([cache_win_k[i][:, DEC_SEQ:], hs(kw)], axis=1),
                     jnp.concatenate([cache_win_v[i][:, DEC_SEQ:], hs(vw)], axis=1), nc_s[:, HIST_OFF:]))

    y_prompt = h_p.reshape(1, SEQ, D_MODEL)
    y_sample = h_s.reshape(DEC_BATCH, DEC_SEQ, D_MODEL)
    outs_p = [jnp.stack([s[n] for s in st_p]) for n in range(7)]
    outs_s = [jnp.stack([s[n] for s in st_s]) for n in range(7)]
    return (y_prompt, y_sample, *outs_p, *outs_s)
```

```python
import functools

import jax
import jax.numpy as jnp
from jax import lax
from jax.experimental import pallas as pl
from jax.experimental.pallas import tpu as pltpu

F32 = jnp.float32
BF16 = jnp.bfloat16

D_MODEL = 2048
SEQ = 8192
DEPTH = 2
DEC_BATCH = 32
DEC_SEQ = 8
PAST_LEN = 8192
PAGE_SIZE = 128
HEAD_DIM = 64
ATTN_W = D_MODEL // 2
C_CONV = D_MODEL - ATTN_W
N_HEADS = ATTN_W // HEAD_DIM
KV_HEADS = N_HEADS // 4
GQA = N_HEADS // KV_HEADS
KV_W = KV_HEADS * HEAD_DIM
ROT_DIM = HEAD_DIM // 4
ROPE_THETA = 500000.0
CONV_WIDTH = 31
CMP_LEN = 32
CMP_STRIDE = 16
SEL_BLOCK = 64
SEL_TOPK = 16
WINDOW = 512
Q_BLOCK = 128
PLE_DIM = 256
EPS = 1e-6
NEG = -1e30
BIG = 1e9

N_SAMPLE = DEC_BATCH * DEC_SEQ
N_PAGES = PAST_LEN // PAGE_SIZE
N_CHUNK = PAST_LEN // CMP_STRIDE
N_CMP = N_CHUNK
PAGE_CHUNKS = PAGE_SIZE // CMP_STRIDE
NSB_P = SEQ // SEL_BLOCK
NSB_S = -(-(PAST_LEN + DEC_SEQ) // SEL_BLOCK)
NSB_S_PAD = 256
BELOW_NEG = -3e38
LOG2E = 1.4426950408889634
SCALE = HEAD_DIM ** -0.5 * LOG2E

LANES = 128
SUBLANES = 8
VMEM_LIMIT = 56 * 1024 * 1024

A_COLS = 4 * C_CONV
B_COLS = ATTN_W + 6 * KV_W
B_ROPE_COLS = ATTN_W + 3 * KV_W
B_KC, B_KS, B_KW, B_VC, B_VS, B_VW = (ATTN_W // KV_W + n for n in range(6))
PROJ_TM = 1024
PROJ_TN = 512
SLC_KT = 1024


def _dot(a, b):
    return jnp.dot(a, b, preferred_element_type=F32)


def _dot_nt(a, b):
    return lax.dot_general(a, b, (((1,), (1,)), ((), ())), preferred_element_type=F32)


def _lane_tile(x, n):
    return x if n == 1 else jnp.concatenate([x] * n, axis=1)


def _sigmoid(x):
    return 1.0 / (1.0 + jnp.exp(-x))


def _inproj_kernel(*refs, rope_chunks, extras):
    if extras:
        x_ref, g_ref, w_ref, wg_ref, cos_ref, sa_ref, sb_ref, u_ref, ub_ref, gate_ref, hn_ref = refs
    else:
        x_ref, g_ref, w_ref, u_ref, hn_ref = refs
    j = pl.program_id(1)

    @pl.when(j == 0)
    def _():
        x = x_ref[...]
        r = lax.rsqrt(jnp.mean(x * x, axis=-1, keepdims=True) + EPS)
        hn = ((x * r) * g_ref[...]).astype(BF16)
        hn_ref[...] = hn
        if extras:
            gate_ref[...] = _dot(hn, wg_ref[...])

    acc = _dot(hn_ref[...], w_ref[...])
    if not extras:
        u_ref[...] = acc
        return

    def store(n_rope):
        c, sa, sb = cos_ref[...], sa_ref[...], sb_ref[...]
        for k in range(acc.shape[1] // LANES):
            cs = slice(k * LANES, (k + 1) * LANES)
            y = acc[:, cs]
            if k < n_rope:
                y = y * c + pltpu.roll(y, LANES - ROT_DIM // 2, 1) * sa + pltpu.roll(y, ROT_DIM // 2, 1) * sb
            u_ref[:, cs] = y
            ub_ref[:, cs] = y.astype(BF16)

    full_tiles, part_chunks = rope_chunks // (acc.shape[1] // LANES), rope_chunks % (acc.shape[1] // LANES)
    pl.when(j < full_tiles)(lambda: store(acc.shape[1] // LANES))
    if part_chunks:
        pl.when(j == full_tiles)(lambda: store(part_chunks))
    pl.when(j >= full_tiles + bool(part_chunks))(lambda: store(0))


def _inproj(x, g, w, extras=None):
    n_rows, n_cols = x.shape[0], w.shape[1]
    tm, tn = min(PROJ_TM, n_rows), PROJ_TN
    grid = (n_rows // tm, n_cols // tn)
    row = lambda i, j: (i, 0)
    in_specs = [pl.BlockSpec((tm, D_MODEL), row),
                pl.BlockSpec((1, D_MODEL), lambda i, j: (0, 0)),
                pl.BlockSpec((D_MODEL, tn), lambda i, j: (0, j))]
    out_specs = [pl.BlockSpec((tm, tn), lambda i, j: (i, j))]
    out_shape = [jax.ShapeDtypeStruct((n_rows, n_cols), F32)]
    args = [x, g, w]
    if extras:
        wg, cos, sa, sb = extras
        in_specs += [pl.BlockSpec((D_MODEL, LANES), lambda i, j: (0, 0))] + [pl.BlockSpec((tm, LANES), row)] * 3
        out_specs += [pl.BlockSpec((tm, tn), lambda i, j: (i, j)), pl.BlockSpec((tm, LANES), row)]
        out_shape += [jax.ShapeDtypeStruct((n_rows, n_cols), BF16), jax.ShapeDtypeStruct((n_rows, LANES), F32)]
        args += [wg, cos, sa, sb]
    return pl.pallas_call(
        functools.partial(_inproj_kernel, rope_chunks=B_ROPE_COLS // LANES, extras=bool(extras)),
        grid=grid, in_specs=in_specs, out_specs=out_specs, out_shape=out_shape,
        scratch_shapes=[pltpu.VMEM((tm, D_MODEL), BF16)],
        compiler_params=pltpu.CompilerParams(dimension_semantics=("parallel", "arbitrary"),
                                             vmem_limit_bytes=VMEM_LIMIT),
        name="inproj_b" if extras else "inproj_a",
    )(*args)


HIST_PAD = 32
HIST_OFF = HIST_PAD - (CONV_WIDTH - 1)
CONV_ROWS = 128
NORM_ROWS = 32


def _conv_kernel(a_ref, b_ref, zc_ref, hist_ref, w_ref, cb_ref, lg_ref, lb_ref, yc_ref, nc_ref,
                 ext_ref, sh_ref, y_ref, *, tm):
    t = pl.program_id(1)

    @pl.when(t == 0)
    def _():
        ext_ref[0:HIST_PAD, :] = hist_ref[0]

    ext_ref[HIST_PAD:HIST_PAD + tm, :] = a_ref[...] * _sigmoid(b_ref[...])
    n_sh = tm + HIST_PAD - SUBLANES
    for s in range(1, SUBLANES):
        sh_ref[s - 1] = ext_ref[s:s + n_sh, :]
    rb = min(tm, CONV_ROWS)
    for c in range(C_CONV // LANES):
        cs = slice(c * LANES, (c + 1) * LANES)
        for r0 in range(0, tm, rb):
            acc = jnp.zeros((rb, LANES), F32) + cb_ref[:, cs]
            for k in range(CONV_WIDTH):
                base, s = (HIST_OFF + k) // SUBLANES * SUBLANES + r0, (HIST_OFF + k) % SUBLANES
                rows = ext_ref[base:base + rb, cs] if s == 0 else sh_ref[s - 1, base:base + rb, cs]
                acc = acc + rows * w_ref[k:k + 1, cs]
            y_ref[r0:r0 + rb, cs] = acc
    nb = min(tm, NORM_ROWS)
    for r0 in range(0, tm, nb):
        rs = slice(r0, r0 + nb)
        y = y_ref[rs, :]
        mu = jnp.mean(y, axis=-1, keepdims=True)
        yc = y - mu
        yn = yc * lax.rsqrt(jnp.mean(yc * yc, axis=-1, keepdims=True) + EPS) * lg_ref[...] + lb_ref[...]
        zc = zc_ref[rs, :]
        yc_ref[rs, :] = ((yn * _sigmoid(yn)) * (zc * _sigmoid(zc))).astype(yc_ref.dtype)
    tail = ext_ref[tm:tm + HIST_PAD, :]
    ext_ref[0:HIST_PAD, :] = tail

    @pl.when(t == pl.num_programs(1) - 1)
    def _():
        nc_ref[0] = tail


def _conv(ua, hist, w, cb, lg, lb, *, batch, t_len, tm):
    nt = t_len // tm
    col = lambda c: (lambda b, t: (b * nt + t, c))
    vec = pl.BlockSpec((1, C_CONV), lambda b, t: (0, 0))
    return pl.pallas_call(
        functools.partial(_conv_kernel, tm=tm),
        grid=(batch, nt),
        in_specs=[pl.BlockSpec((tm, C_CONV), col(0)), pl.BlockSpec((tm, C_CONV), col(1)),
                  pl.BlockSpec((tm, C_CONV), col(2)),
                  pl.BlockSpec((1, HIST_PAD, C_CONV), lambda b, t: (b, 0, 0)),
                  pl.BlockSpec((CONV_WIDTH, C_CONV), lambda b, t: (0, 0)), vec, vec, vec],
        out_specs=[pl.BlockSpec((tm, C_CONV), lambda b, t: (b * nt + t, 0)),
                   pl.BlockSpec((1, HIST_PAD, C_CONV), lambda b, t: (b, 0, 0))],
        out_shape=[jax.ShapeDtypeStruct((batch * t_len, C_CONV), BF16 if tm % 16 == 0 else F32),
                   jax.ShapeDtypeStruct((batch, HIST_PAD, C_CONV), F32)],
        scratch_shapes=[pltpu.VMEM((HIST_PAD + tm, C_CONV), F32),
                        pltpu.VMEM((SUBLANES - 1, tm + HIST_PAD - SUBLANES, C_CONV), F32),
                        pltpu.VMEM((tm, C_CONV), F32)],
        compiler_params=pltpu.CompilerParams(dimension_semantics=("parallel", "arbitrary"),
                                             vmem_limit_bytes=VMEM_LIMIT),
        name="conv_group",
    )(ua, ua, ua, hist, w, cb, lg, lb)


CMP_ROWS = PAST_LEN + CMP_STRIDE


LANE_HALVES = KV_W // LANES


def _compress_rows(x_ref, w_ref):
    acc = None
    for l in range(CMP_LEN):
        xl = jnp.concatenate([x_ref[c, pl.ds(l, N_CMP, stride=CMP_STRIDE), :] for c in range(LANE_HALVES)], axis=1)
        part = _dot(xl.astype(BF16), w_ref[l])
        acc = part if acc is None else acc + part
    return acc


def _pcompress_kernel(x_ref, w_ref, o_ref, rows_ref):
    for c in range(LANE_HALVES):
        rows_ref[c, 0:SEQ, :] = x_ref[:, c * LANES:(c + 1) * LANES]
        rows_ref[c, SEQ:, :] = jnp.zeros((CMP_ROWS - SEQ, LANES), F32)
    o_ref[...] = _compress_rows(rows_ref, w_ref).astype(o_ref.dtype)


def _pcompress(ub, col_block, w):
    once = pl.Buffered(1)
    return pl.pallas_call(
        _pcompress_kernel,
        grid=(1,),
        in_specs=[pl.BlockSpec((SEQ, KV_W), lambda i: (0, col_block), pipeline_mode=once),
                  pl.BlockSpec((CMP_LEN, KV_W, KV_W), lambda i: (0, 0, 0), pipeline_mode=once)],
        out_specs=pl.BlockSpec((N_CMP, KV_W), lambda i: (0, 0)),
        out_shape=jax.ShapeDtypeStruct((N_CMP, KV_W), BF16),
        scratch_shapes=[pltpu.VMEM((LANE_HALVES, CMP_ROWS, LANES), F32)],
        compiler_params=pltpu.CompilerParams(dimension_semantics=("arbitrary",), vmem_limit_bytes=VMEM_LIMIT),
        name="compress_prompt",
    )(ub, w)


S_CHUNKS = N_CHUNK + SUBLANES


def _scompress_kernel(pt_ref, cache_ref, new_ref, perm_ref, wab_ref, o_ref, tbuf, xc_ref, sem, *, layer):
    b = pl.program_id(0)
    slot = b % 2

    def copies(bb, sl, go):
        def body(pp, carry):
            for k in range(2):
                p = 2 * pp + k
                cp = pltpu.make_async_copy(cache_ref.at[layer, pt_ref[bb, p]],
                                           tbuf.at[sl, :, pl.ds(pl.multiple_of(p * PAGE_SIZE, PAGE_SIZE), PAGE_SIZE)],
                                           sem.at[sl])
                cp.start(priority=k) if go else cp.wait()
            return carry
        lax.fori_loop(0, N_PAGES // 2, body, 0)

    @pl.when(b == 0)
    def _():
        copies(0, 0, True)

    @pl.when(b + 1 < pl.num_programs(0))
    def _():
        copies(b + 1, 1 - slot, True)

    copies(b, slot, False)
    perm = perm_ref[...]
    for pp in range(N_PAGES // 2):
        pages = tbuf[slot, :, pp * 2 * PAGE_SIZE:(pp + 1) * 2 * PAGE_SIZE].T.astype(BF16)
        rows = _dot(perm, pages)
        for k in range(2):
            p = 2 * pp + k
            for l in range(CMP_STRIDE):
                r0 = k * PAGE_SIZE + l * PAGE_CHUNKS
                for c in range(LANE_HALVES):
                    xc_ref[c, p * PAGE_CHUNKS:(p + 1) * PAGE_CHUNKS, l * LANES:(l + 1) * LANES] = (
                        rows[r0:r0 + PAGE_CHUNKS, c * LANES:(c + 1) * LANES])
    for c in range(LANE_HALVES):
        cs = slice(c * LANES, (c + 1) * LANES)
        xc_ref[c, N_CHUNK:, :] = jnp.zeros((SUBLANES, CMP_STRIDE * LANES), F32)
        for l in range(DEC_SEQ):
            xc_ref[c, N_CHUNK:N_CHUNK + 1, l * LANES:(l + 1) * LANES] = new_ref[0, l:l + 1, cs]
        z = _dot(xc_ref[c].astype(BF16), wab_ref[c])
        zb = pltpu.roll(z[:, LANES:], S_CHUNKS - 1, 0)
        o_ref[0, :, cs] = (z[0:N_CMP, :LANES] + zb[0:N_CMP]).astype(o_ref.dtype)


def _scompress(page_table, cache_t, new_rows, perm, wab, layer):
    once = pl.Buffered(1)
    return pl.pallas_call(
        functools.partial(_scompress_kernel, layer=layer),
        grid_spec=pltpu.PrefetchScalarGridSpec(
            num_scalar_prefetch=1, grid=(DEC_BATCH,),
            in_specs=[pl.BlockSpec(memory_space=pl.ANY),
                      pl.BlockSpec((1, DEC_SEQ, KV_W), lambda b, pt: (b, 0, 0)),
                      pl.BlockSpec((2 * PAGE_SIZE, 2 * PAGE_SIZE), lambda b, pt: (0, 0), pipeline_mode=once),
                      pl.BlockSpec((LANE_HALVES, CMP_STRIDE * LANES, 2 * LANES), lambda b, pt: (0, 0, 0),
                                   pipeline_mode=once)],
            out_specs=pl.BlockSpec((1, N_CMP, KV_W), lambda b, pt: (b, 0, 0)),
            scratch_shapes=[pltpu.VMEM((2, KV_W, PAST_LEN), F32),
                            pltpu.VMEM((LANE_HALVES, S_CHUNKS, CMP_STRIDE * LANES), F32),
                            pltpu.SemaphoreType.DMA((2,))]),
        out_shape=jax.ShapeDtypeStruct((DEC_BATCH, N_CMP, KV_W), BF16),
        compiler_params=pltpu.CompilerParams(dimension_semantics=("arbitrary",), vmem_limit_bytes=VMEM_LIMIT),
        name="compress_sample",
    )(page_table, cache_t, new_rows, perm, wab)


def _cmp_branch(q, tcol, ck, cv):
    s = _dot_nt(q, ck)
    n = lax.broadcasted_iota(jnp.int32, s.shape, 1)
    vis = (n * CMP_STRIDE + (CMP_LEN - 1)) <= tcol
    sm = jnp.where(vis, s, NEG)
    e = jnp.where(vis, jnp.exp2(sm - jnp.max(sm, axis=1, keepdims=True)), 0.0)
    den = jnp.sum(e, axis=1, keepdims=True)
    p = e / jnp.where(den > 0.0, den, 1.0)
    return _dot(p.astype(BF16), cv), p


def _select_bias(imp, t, nsb, axis):
    j = lax.broadcasted_iota(jnp.int32, imp.shape, axis)
    cur = lax.shift_right_arithmetic(t, 6)
    forced = (j == 0) | (j == cur) | (j == cur - 1)
    score = jnp.where(j * SEL_BLOCK <= t, jnp.where(forced, BIG, imp), NEG)
    score = jnp.where(j < nsb, score, BELOW_NEG)
    jf = j.astype(F32)
    bias = jnp.full(imp.shape, NEG, F32)
    for _ in range(SEL_TOPK):
        mx = jnp.max(score, axis=axis, keepdims=True)
        idx = jnp.min(jnp.where(score == mx, jf, 1e9), axis=axis, keepdims=True)
        pick = jf == idx
        bias = jnp.where(pick, 0.0, bias)
        score = jnp.where(pick, BELOW_NEG, score)
    return bias


def _block_onehot(nb, kt, first_block):
    j = lax.broadcasted_iota(jnp.int32, (nb, kt), 0)
    col = lax.broadcasted_iota(jnp.int32, (nb, kt), 1)
    return jnp.where(j - lax.shift_right_arithmetic(col, 6) == first_block, 1.0, 0.0).astype(BF16)


def _online_update(s, v, m_ref, l_ref, acc_ref, rows, v_transposed=False):
    m_prev, l_prev = m_ref[rows, :], l_ref[rows, :]
    m_next = jnp.maximum(m_prev, jnp.max(s, axis=1, keepdims=True))
    alpha = jnp.exp2(m_prev - m_next)
    p = jnp.exp2(s - _lane_tile(m_next, s.shape[1] // LANES))
    l_ref[rows, :] = alpha * l_prev + jnp.sum(p, axis=1, keepdims=True)
    m_ref[rows, :] = m_next
    pv = _dot_nt(p.astype(BF16), v) if v_transposed else _dot(p.astype(BF16), v)
    acc_ref[rows, :] = acc_ref[rows, :] * _lane_tile(alpha, KV_W // LANES) + pv


def _window_mask(tcol, kpos):
    return lax.bitcast_convert_type(tcol - kpos, jnp.uint32) <= jnp.uint32(WINDOW)


GROUP_ROWS = GQA * Q_BLOCK


PACK_TM = 512


def _pack_keys_kernel(k_ref, o_ref):
    i = pl.program_id(0)
    lane = lax.broadcasted_iota(jnp.int32, (PACK_TM, LANES), 1)
    blk = lax.shift_right_arithmetic(i * PACK_TM + lax.broadcasted_iota(jnp.int32, (PACK_TM, LANES), 0), 6)
    lo = lane < HEAD_DIM
    hot_lo = jnp.where(lane - HEAD_DIM == blk, 1.0, 0.0)
    hot_hi = jnp.where(lane + HEAD_DIM == blk, 1.0, 0.0)
    for g in range(KV_HEADS):
        pair = k_ref[:, (g // 2) * LANES:(g // 2 + 1) * LANES]
        if g % 2:
            pair = pltpu.roll(pair, HEAD_DIM, 1)
        o_ref[g, :, 0:LANES] = jnp.where(lo, pair, hot_lo).astype(BF16)
        o_ref[g, :, LANES:2 * LANES] = jnp.where(lo, hot_hi, 0.0).astype(BF16)


def _pack_keys(ub, col_block):
    assert NSB_P == 2 * HEAD_DIM and KV_W == 2 * LANES
    return pl.pallas_call(
        _pack_keys_kernel,
        grid=(SEQ // PACK_TM,),
        in_specs=[pl.BlockSpec((PACK_TM, KV_W), lambda i: (i, col_block))],
        out_specs=pl.BlockSpec((KV_HEADS, PACK_TM, KV_W), lambda i: (0, i, 0)),
        out_shape=jax.ShapeDtypeStruct((KV_HEADS, SEQ, KV_W), BF16),
        compiler_params=pltpu.CompilerParams(dimension_semantics=("parallel",), vmem_limit_bytes=VMEM_LIMIT),
        name="pack_keys",
    )(ub)


def _pattn_kernel(q_ref, ck_ref, cv_ref, mm_ref, ka_ref, vs_ref, kw_ref, vw_ref, gt_ref, za_ref, o_ref,
                  qbd_ref, qa_ref, oc_ref, acc_ref, m_ref, l_ref):
    i = pl.program_id(0)
    q0 = i * Q_BLOCK
    T, R = Q_BLOCK, GROUP_ROWS
    lane = lax.broadcasted_iota(jnp.int32, (T, LANES), 1)
    half = lax.shift_right_arithmetic(lane, 6)
    lo = lane < HEAD_DIM
    tcol = q0 + (lax.broadcasted_iota(jnp.int32, (R, 1), 0) & (T - 1))
    trow = q0 + lax.broadcasted_iota(jnp.int32, (1, T), 1)

    zeros = jnp.zeros((T, LANES), BF16)
    for hh in range(N_HEADS):
        g = hh // GQA
        pair = q_ref[:, (hh // 2) * LANES:(hh // 2 + 1) * LANES] * SCALE
        if hh % 2 != g % 2:
            pair = pltpu.roll(pair, HEAD_DIM, 1)
        x = jnp.where(half == g % 2, pair, 0.0).astype(BF16)
        qbd_ref[hh * T:(hh + 1) * T, 0:LANES] = x if g // 2 == 0 else zeros
        qbd_ref[hh * T:(hh + 1) * T, LANES:2 * LANES] = zeros if g // 2 == 0 else x

    for g in range(KV_HEADS):
        rows = slice(g * R, (g + 1) * R)
        o_c, p = _cmp_branch(qbd_ref[rows, :], tcol, ck_ref[...], cv_ref[...])
        oc_ref[rows, :] = o_c
        imp_rows = _dot(p.astype(BF16), mm_ref[...])
        imp = imp_rows[0:T] + imp_rows[T:2 * T] + imp_rows[2 * T:3 * T] + imp_rows[3 * T:4 * T]
        bias = _select_bias(imp.T, trow, NSB_P, axis=0).T
        bias = pltpu.roll(bias, HEAD_DIM, 1)
        for h in range(GQA):
            hh = g * GQA + h
            pair = q_ref[:, (hh // 2) * LANES:(hh // 2 + 1) * LANES] * SCALE
            if hh % 2:
                pair = pltpu.roll(pair, HEAD_DIM, 1)
            qa_ref[hh * T:(hh + 1) * T, 0:LANES] = jnp.where(lo, pair, bias).astype(BF16)
            qa_ref[hh * T:(hh + 1) * T, LANES:2 * LANES] = jnp.where(lo, bias, 0.0).astype(BF16)

    m_ref[...] = jnp.full(m_ref.shape, NEG, F32)
    l_ref[...] = jnp.zeros(l_ref.shape, F32)
    acc_ref[...] = jnp.zeros(acc_ref.shape, F32)
    kcol = lax.broadcasted_iota(jnp.int32, (1, SLC_KT), 1)

    def slc_tile(c, causal):
        k0 = pl.multiple_of(c * SLC_KT, SLC_KT)
        v_t = vs_ref[pl.ds(k0, SLC_KT), :]
        for g in range(KV_HEADS):
            rows = slice(g * R, (g + 1) * R)
            s = _dot_nt(qa_ref[rows, :], ka_ref[g, pl.ds(k0, SLC_KT), :])
            if causal:
                s = jnp.where(k0 + kcol <= tcol, s, NEG)
            _online_update(s, v_t, m_ref, l_ref, acc_ref, rows)

    c_last = (q0 + T - 1) // SLC_KT

    def full_tile(c, carry):
        slc_tile(c, False)
        return carry

    lax.fori_loop(0, c_last, full_tile, 0)
    slc_tile(c_last, True)

    start = pl.multiple_of(jnp.maximum(q0 - WINDOW, 0), Q_BLOCK)
    kw_t = kw_ref[pl.ds(start, WINDOW + T), :]
    vw_t = vw_ref[pl.ds(start, WINDOW + T), :]
    kpos = start + lax.broadcasted_iota(jnp.int32, (1, WINDOW + T), 1)
    sig = _sigmoid(gt_ref[...])
    for g in range(KV_HEADS):
        rows = slice(g * R, (g + 1) * R)
        cc = slice((g // 2) * LANES, (g // 2 + 1) * LANES)
        s3 = jnp.where(_window_mask(tcol, kpos), _dot_nt(qbd_ref[rows, :], kw_t), NEG)
        e3 = jnp.exp2(s3 - jnp.max(s3, axis=1, keepdims=True))
        o_w = _dot(e3.astype(BF16), vw_t)[:, cc] / jnp.sum(e3, axis=1, keepdims=True)
        o_s = acc_ref[rows, cc] / l_ref[rows, :]
        o_c = oc_ref[rows, cc]
        for pair in range(2):
            pieces = []
            for sub in range(2):
                h = 2 * pair + sub
                hh = g * GQA + h
                hr = slice(h * T, (h + 1) * T)
                comb = (sig[:, hh:hh + 1] * o_c[hr] + sig[:, N_HEADS + hh:N_HEADS + hh + 1] * o_s[hr]
                        + sig[:, 2 * N_HEADS + hh:2 * N_HEADS + hh + 1] * o_w[hr])
                pieces.append(comb if g % 2 == sub else pltpu.roll(comb, HEAD_DIM, 1))
            oc_cols = slice((2 * g + pair) * LANES, (2 * g + pair + 1) * LANES)
            za = za_ref[:, oc_cols]
            o_ref[:, oc_cols] = (jnp.where(half == 0, pieces[0], pieces[1]) * (za * _sigmoid(za))).astype(BF16)


def _pattn(ub, ck, cv, mmat, kaug, ubb, gates, ua):
    const = lambda r, c: (lambda i: (r, c))
    once = pl.Buffered(1)
    resident = lambda c: pl.BlockSpec((SEQ, KV_W), const(0, c), pipeline_mode=once)
    small = lambda shape: pl.BlockSpec(shape, const(0, 0))
    return pl.pallas_call(
        _pattn_kernel,
        grid=(SEQ // Q_BLOCK,),
        in_specs=[pl.BlockSpec((Q_BLOCK, ATTN_W), lambda i: (i, 0)),
                  small((N_CMP, KV_W)), small((N_CMP, KV_W)), small((N_CMP, NSB_P)),
                  pl.BlockSpec((KV_HEADS, SEQ, KV_W), lambda i: (0, 0, 0), pipeline_mode=once),
                  resident(B_VS), resident(B_KW), resident(B_VW),
                  pl.BlockSpec((Q_BLOCK, LANES), lambda i: (i, 0)),
                  pl.BlockSpec((Q_BLOCK, ATTN_W), lambda i: (i, 3))],
        out_specs=pl.BlockSpec((Q_BLOCK, ATTN_W), lambda i: (i, 0)),
        out_shape=jax.ShapeDtypeStruct((SEQ, ATTN_W), BF16),
        scratch_shapes=[pltpu.VMEM((N_HEADS * Q_BLOCK, KV_W), BF16),
                        pltpu.VMEM((N_HEADS * Q_BLOCK, KV_W), BF16),
                        pltpu.VMEM((N_HEADS * Q_BLOCK, KV_W), F32),
                        pltpu.VMEM((N_HEADS * Q_BLOCK, KV_W), F32),
                        pltpu.VMEM((N_HEADS * Q_BLOCK, LANES), F32),
                        pltpu.VMEM((N_HEADS * Q_BLOCK, LANES), F32)],
        compiler_params=pltpu.CompilerParams(dimension_semantics=("parallel",), vmem_limit_bytes=VMEM_LIMIT),
        name="attn_prompt",
    )(ub, ck, cv, mmat, kaug, ubb, ubb, ubb, gates, ua)


SLC_PAGES = 16
SLC_CHUNKS = N_PAGES // SLC_PAGES
SLC_CHUNK_ROWS = SLC_PAGES * PAGE_SIZE
S_ROWS = N_HEADS * DEC_SEQ
NEW_PAD = LANES


def _sattn_kernel(pt_ref, qt_ref, ck_ref, cv_ref, mm_ref, ksc_ref, vsc_ref, ksn_ref, vsn_ref,
                  kwc_ref, vwc_ref, kwn_ref, vwn_ref, gt_ref, za_ref, o_ref,
                  kbuf, vbuf, ksem, vsem, newk_ref, newv_ref, s_ref, *, layer):
    b = pl.program_id(0)
    nb = pl.num_programs(0)
    R = S_ROWS
    row = lax.broadcasted_iota(jnp.int32, (R, 1), 0)
    tcol = PAST_LEN + (row & (DEC_SEQ - 1))
    lane_group = lax.shift_right_arithmetic(lax.broadcasted_iota(jnp.int32, (R, KV_W), 1), 6)
    diag = lane_group == lax.shift_right_arithmetic(row, 5)
    vslot = b % 2

    def page_copies(cache, buf, sem, bb, first_page, n_pages, slot, go, priority):
        def body(p, carry):
            cp = pltpu.make_async_copy(cache.at[layer, pt_ref[bb, first_page + p]],
                                       buf.at[slot, :, pl.ds(pl.multiple_of(p * PAGE_SIZE, PAGE_SIZE), PAGE_SIZE)],
                                       sem.at[slot])
            cp.start(priority=priority) if go else cp.wait()
            return carry
        lax.fori_loop(0, n_pages, body, 0)

    def key_pages(bb, slot, go):
        page_copies(ksc_ref, kbuf, ksem, bb, 0, N_PAGES, slot, go, 0)

    def value_pages(bb, slot, go):
        page_copies(vsc_ref, vbuf, vsem, bb, 0, N_PAGES, slot, go, 1)

    @pl.when(b == 0)
    def _():
        key_pages(0, 0, True)
        value_pages(0, 0, True)

    @pl.when(b + 1 < nb)
    def _():
        key_pages(b + 1, 1 - vslot, True)
        value_pages(b + 1, 1 - vslot, True)

    q = jnp.where(diag, qt_ref[0] * SCALE, 0.0).astype(BF16)

    o_c, p = _cmp_branch(q, tcol, ck_ref[0], cv_ref[0])
    T = DEC_SEQ
    imp_rows = _dot(p.astype(BF16), mm_ref[...])
    groups = []
    for r0 in range(0, R, GQA * T):
        grp = imp_rows[r0:r0 + T] + imp_rows[r0 + T:r0 + 2 * T] + imp_rows[r0 + 2 * T:r0 + 3 * T] + imp_rows[r0 + 3 * T:r0 + 4 * T]
        groups += [grp] * GQA
    imp = jnp.concatenate(groups, axis=0)
    trow = PAST_LEN + (lax.broadcasted_iota(jnp.int32, (1, R), 1) & (T - 1))
    a = _select_bias(imp.T, trow, NSB_S, axis=0).T.astype(BF16)

    def lane_blocks(x):
        return [x[:, j * LANES:(j + 1) * LANES] for j in range(x.shape[1] // LANES)]

    m_lanes = jnp.full((R, LANES), NEG, F32)
    key_pages(b, vslot, False)
    for c in range(SLC_CHUNKS):
        cs = slice(c * SLC_CHUNK_ROWS, (c + 1) * SLC_CHUNK_ROWS)
        onehot = _block_onehot(NSB_S_PAD, SLC_CHUNK_ROWS, c * SLC_CHUNK_ROWS // SEL_BLOCK)
        s = _dot(q, kbuf[vslot, :, cs].astype(BF16)) + _dot(a, onehot)
        s_ref[:, cs] = s
        for blk in lane_blocks(s):
            m_lanes = jnp.maximum(m_lanes, blk)

    newcol = lax.broadcasted_iota(jnp.int32, (1, NEW_PAD), 1)
    new_ok = PAST_LEN + newcol <= tcol
    newk_ref[...] = jnp.zeros(newk_ref.shape, BF16)
    newv_ref[...] = jnp.zeros(newv_ref.shape, BF16)
    newk_ref[0:T, :] = ksn_ref[0].astype(BF16)
    newv_ref[0:T, :] = vsn_ref[0].astype(BF16)
    onehot = _block_onehot(NSB_S_PAD, NEW_PAD, PAST_LEN // SEL_BLOCK)
    s_new = jnp.where(new_ok, _dot_nt(q, newk_ref[...]) + _dot(a, onehot), NEG)
    m = jnp.max(jnp.maximum(m_lanes, s_new), axis=1, keepdims=True)

    value_pages(b, vslot, False)
    p_new = jnp.exp2(s_new - m)
    l_lanes = p_new
    acc = _dot(p_new.astype(BF16), newv_ref[...])
    for c in range(SLC_CHUNKS):
        cs = slice(c * SLC_CHUNK_ROWS, (c + 1) * SLC_CHUNK_ROWS)
        p = jnp.exp2(s_ref[:, cs] - m)
        for blk in lane_blocks(p):
            l_lanes = l_lanes + blk
        acc = acc + _dot_nt(p.astype(BF16), vbuf[vslot, :, cs].astype(BF16))
    o_s = acc / jnp.sum(l_lanes, axis=1, keepdims=True)

    wpos = (PAST_LEN - WINDOW) + lax.broadcasted_iota(jnp.int32, (1, WINDOW), 1)
    newk_ref[0:T, :] = kwn_ref[0].astype(BF16)
    newv_ref[0:T, :] = vwn_ref[0].astype(BF16)
    s3 = jnp.concatenate(
        [jnp.where(_window_mask(tcol, wpos), _dot(q, kwc_ref[0, 0].astype(BF16)), NEG),
         jnp.where(new_ok, _dot_nt(q, newk_ref[...]), NEG)], axis=1)
    e3 = jnp.exp2(s3 - jnp.max(s3, axis=1, keepdims=True))
    p3 = e3.astype(BF16)
    o_w = (_dot_nt(p3[:, 0:WINDOW], vwc_ref[0, 0].astype(BF16)) + _dot(p3[:, WINDOW:], newv_ref[...]))
    o_w = o_w / jnp.sum(e3, axis=1, keepdims=True)

    sig = _sigmoid(gt_ref[0])
    comb = jnp.where(diag, sig[:, 0:1] * o_c + sig[:, 1:2] * o_s + sig[:, 2:3] * o_w, 0.0)
    folded = comb[:, 0:LANES] + comb[:, LANES:2 * LANES]
    folded = folded + pltpu.roll(folded, HEAD_DIM, 1)
    za = za_ref[0]
    o_ref[0] = folded[:, 0:HEAD_DIM] * (za * _sigmoid(za))


def _sattn(page_table, qt, ck, cv, mmat, cache_k, cache_v, ksn, vsn, win_k, win_v, kwn, vwn, gates, za, layer):
    per_b = lambda shape: pl.BlockSpec((1,) + shape, lambda b, pt: (b,) + (0,) * len(shape))
    win = pl.BlockSpec((1, 1, KV_W, WINDOW), lambda b, pt: (layer, b, 0, 0))
    hbm = pl.BlockSpec(memory_space=pl.ANY)
    return pl.pallas_call(
        functools.partial(_sattn_kernel, layer=layer),
        grid_spec=pltpu.PrefetchScalarGridSpec(
            num_scalar_prefetch=1, grid=(DEC_BATCH,),
            in_specs=[per_b((S_ROWS, KV_W)), per_b((N_CMP, KV_W)), per_b((N_CMP, KV_W)),
                      pl.BlockSpec((N_CMP, NSB_S_PAD), lambda b, pt: (0, 0)),
                      hbm, hbm, per_b((DEC_SEQ, KV_W)), per_b((DEC_SEQ, KV_W)),
                      win, win, per_b((DEC_SEQ, KV_W)), per_b((DEC_SEQ, KV_W)),
                      per_b((S_ROWS, LANES)), per_b((S_ROWS, HEAD_DIM))],
            out_specs=per_b((S_ROWS, HEAD_DIM)),
            scratch_shapes=[pltpu.VMEM((2, KV_W, PAST_LEN), F32), pltpu.VMEM((2, KV_W, PAST_LEN), F32),
                            pltpu.SemaphoreType.DMA((2,)), pltpu.SemaphoreType.DMA((2,)),
                            pltpu.VMEM((NEW_PAD, KV_W), BF16), pltpu.VMEM((NEW_PAD, KV_W), BF16),
                            pltpu.VMEM((S_ROWS, PAST_LEN), F32)]),
        out_shape=jax.ShapeDtypeStruct((DEC_BATCH, S_ROWS, HEAD_DIM), F32),
        compiler_params=pltpu.CompilerParams(dimension_semantics=("arbitrary",), vmem_limit_bytes=VMEM_LIMIT),
        name="attn_sample",
    )(page_table, qt, ck, cv, mmat, cache_k, cache_v, ksn, vsn, win_k, win_v, kwn, vwn, gates, za)


OUT_TM = 512


def _outproj_kernel(h_ref, yc_ref, ya_ref, pe_ref, wc_ref, wa_ref, wg_ref, wp_ref, fg_ref, o_ref, *, final):
    h1 = h_ref[...] + (_dot(yc_ref[...], wc_ref[...]) + _dot(ya_ref[...], wa_ref[...]))
    h2 = h1 + _sigmoid(_dot(h1.astype(BF16), wg_ref[...])) * _dot(pe_ref[...].astype(BF16), wp_ref[...])
    if final:
        h2 = (h2 * lax.rsqrt(jnp.mean(h2 * h2, axis=-1, keepdims=True) + EPS)) * fg_ref[...]
    o_ref[...] = h2


def _outproj(h, yc, ya, pe, wc, wa, wg, wp, fg, final):
    n_rows = h.shape[0]
    tm = min(OUT_TM, n_rows)
    row = lambda w: pl.BlockSpec((tm, w), lambda i: (i, 0))
    const = lambda shape: pl.BlockSpec(shape, lambda i: (0, 0), pipeline_mode=pl.Buffered(1))
    return pl.pallas_call(
        functools.partial(_outproj_kernel, final=final),
        grid=(n_rows // tm,),
        in_specs=[row(D_MODEL), row(C_CONV), row(ATTN_W), row(PLE_DIM),
                  const((C_CONV, D_MODEL)), const((ATTN_W, D_MODEL)), const((D_MODEL, D_MODEL)),
                  const((PLE_DIM, D_MODEL)), const((1, D_MODEL))],
        out_specs=row(D_MODEL),
        out_shape=jax.ShapeDtypeStruct((n_rows, D_MODEL), F32),
        compiler_params=pltpu.CompilerParams(dimension_semantics=("parallel",), vmem_limit_bytes=VMEM_LIMIT),
        name="outproj_final" if final else "outproj",
    )(h, yc, ya, pe, wc, wa, wg, wp, fg)


def _rope_tables(pos):
    n_rows = pos.shape[0]
    half = ROT_DIM // 2
    inv = ROPE_THETA ** (-2.0 * jnp.arange(half, dtype=F32) / ROT_DIM)
    ang = pos.astype(F32)[:, None] * inv[None, :]
    cos, sin = jnp.cos(ang), jnp.sin(ang)
    one = jnp.ones((n_rows, HEAD_DIM - ROT_DIM), F32)
    zero = lambda n: jnp.zeros((n_rows, n), F32)
    c = jnp.concatenate([cos, cos, one], axis=1)
    sa = jnp.concatenate([-sin, zero(HEAD_DIM - half)], axis=1)
    sb = jnp.concatenate([zero(half), sin, zero(HEAD_DIM - ROT_DIM)], axis=1)
    return tuple(jnp.concatenate([t, t], axis=1) for t in (c, sa, sb))


def _cmp_to_sel(nsb, pad):
    n = jnp.arange(N_CMP)[:, None] * CMP_STRIDE
    j = jnp.arange(pad)[None, :]
    m = (n < (j + 1) * SEL_BLOCK) & (n + CMP_LEN > j * SEL_BLOCK) & (j < nsb)
    return m.astype(BF16)


def _expand_cmp_weight(w):
    w3 = w.reshape(CMP_LEN, HEAD_DIM, HEAD_DIM)
    eye = jnp.eye(KV_HEADS, dtype=w.dtype)
    return jnp.einsum('lde,gh->lgdhe', w3, eye).reshape(CMP_LEN, KV_W, KV_W).astype(BF16)


def _pages_transposed(cache):
    d, n, p = cache.shape[:3]
    return cache.transpose(0, 1, 3, 4, 2).reshape(d, n, KV_W, p)


def _to_head_rows(x, width):
    return x.reshape(DEC_BATCH, DEC_SEQ, N_HEADS, width).transpose(0, 2, 1, 3).reshape(DEC_BATCH, S_ROWS, width)


def kernel(x_prompt, x_sample, cache_cmp_k, cache_cmp_v, cache_slc_k, cache_slc_v, cache_win_k, cache_win_v,
           state_conv, page_table, p_prompt, p_sample, norm_g, w_in, conv_w, conv_b, conv_ln_g, conv_ln_b,
           w_cmp_k, w_cmp_v, w_out, w_ple, w_ple_gate, final_g):
    h_p, h_s = x_prompt.reshape(SEQ, D_MODEL), x_sample.reshape(N_SAMPLE, D_MODEL)
    rope_p = _rope_tables(jnp.arange(SEQ))
    rope_s = _rope_tables(PAST_LEN + jnp.tile(jnp.arange(DEC_SEQ), DEC_BATCH))
    mm_p = _cmp_to_sel(NSB_P, NSB_P)
    mm_s = _cmp_to_sel(NSB_S, NSB_S_PAD)
    cck, ccv = _pages_transposed(cache_cmp_k), _pages_transposed(cache_cmp_v)
    csk, csv = _pages_transposed(cache_slc_k), _pages_transposed(cache_slc_v)
    cwk, cwv = _pages_transposed(cache_win_k), _pages_transposed(cache_win_v)
    page_row = jnp.arange(2 * PAGE_SIZE)
    in_page = page_row % PAGE_SIZE
    perm_col = (page_row - in_page) + (in_page % CMP_STRIDE) * PAGE_CHUNKS + in_page // CMP_STRIDE
    perm = (page_row[:, None] == perm_col[None, :]).astype(BF16)
    hist_p = jnp.zeros((1, HIST_PAD, C_CONV), F32)
    hist_s = jnp.pad(state_conv, ((0, 0), (0, 0), (HIST_OFF, 0), (0, 0)))

    o_q = 3 * C_CONV
    o_kv = o_q + ATTN_W
    o_gt = o_kv + 6 * KV_W
    o_za = o_gt + 3 * N_HEADS
    kv = lambda wi, n: wi[:, o_kv + n * KV_W:o_kv + (n + 1) * KV_W]

    st_p, st_s = [], []
    for i in range(DEPTH):
        wi = w_in[i]
        w_a = jnp.concatenate([wi[:, :o_q], wi[:, o_za:]], axis=1).astype(BF16)
        w_b = jnp.concatenate([wi[:, o_q:o_kv], kv(wi, 0), kv(wi, 2), kv(wi, 4), kv(wi, 1), kv(wi, 3), kv(wi, 5)],
                              axis=1).astype(BF16)
        w_g = jnp.pad(wi[:, o_gt:o_za], ((0, 0), (0, LANES - 3 * N_HEADS))).astype(BF16)
        g_row = norm_g[i].reshape(1, D_MODEL)

        conv_args = (conv_w[i], conv_b[i].reshape(1, C_CONV), conv_ln_g[i].reshape(1, C_CONV),
                     conv_ln_b[i].reshape(1, C_CONV))
        wab_k, wab_v = _expand_cmp_weight(w_cmp_k[i]), _expand_cmp_weight(w_cmp_v[i])
        wo = w_out[i].astype(BF16)
        out_w = (wo[:C_CONV], wo[C_CONV:], w_ple_gate[i].astype(BF16), w_ple[i].astype(BF16), final_g.reshape(1, D_MODEL))
        final = i == DEPTH - 1
        seg = lambda u, n: u[:, (B_KC + n) * KV_W:(B_KC + n + 1) * KV_W]
        heads = lambda x, b, t: x.reshape(b, t, KV_HEADS, HEAD_DIM)

        ua, = _inproj(h_p, g_row, w_a)
        ub, ubb, gates = _inproj(h_p, g_row, w_b, extras=(w_g, *rope_p))
        yc, nc_p = _conv(ua, hist_p, *conv_args, batch=1, t_len=SEQ, tm=256)
        ck = _pcompress(ub, B_KC, wab_k)
        cv = _pcompress(ub, B_VC, wab_v)
        ya = _pattn(ub, ck, cv, mm_p, _pack_keys(ub, B_KS), ubb, gates, ua)
        h_p = _outproj(h_p, yc, ya, p_prompt[i].reshape(SEQ, PLE_DIM), *out_w, final=final)
        kc, ks, kw, vc, vs, vw = (seg(ub, n) for n in range(6))
        wb = min(WINDOW, SEQ)
        st_p.append((heads(kc, 1, SEQ), heads(vc, 1, SEQ), heads(ks, 1, SEQ), heads(vs, 1, SEQ),
                     heads(kw[SEQ - wb:], 1, wb), heads(vw[SEQ - wb:], 1, wb), nc_p[:, HIST_OFF:]))

        ua, = _inproj(h_s, g_row, w_a)
        ub, _, gates = _inproj(h_s, g_row, w_b, extras=(w_g, *rope_s))
        yc, nc_s = _conv(ua, hist_s[i], *conv_args, batch=DEC_BATCH, t_len=DEC_SEQ, tm=DEC_SEQ)
        kc, ks, kw, vc, vs, vw = (seg(ub, n).reshape(DEC_BATCH, DEC_SEQ, KV_W) for n in range(6))
        halves = lambda w: jnp.stack([jnp.concatenate(
            [w[s:s + CMP_STRIDE, c * LANES:(c + 1) * LANES, c * LANES:(c + 1) * LANES].reshape(CMP_STRIDE * LANES, LANES)
             for s in (0, CMP_STRIDE)], axis=1) for c in range(LANE_HALVES)])
        ck = _scompress(page_table, cck, kc, perm, halves(wab_k), i)
        cv = _scompress(page_table, ccv, vc, perm, halves(wab_v), i)
        qt = jnp.tile(_to_head_rows(ub[:, :ATTN_W], HEAD_DIM), (1, 1, KV_HEADS))
        gt = gates[:, :3 * N_HEADS].reshape(DEC_BATCH, DEC_SEQ, 3, N_HEADS).transpose(0, 3, 1, 2)
        gt = jnp.pad(gt.reshape(DEC_BATCH, S_ROWS, 3), ((0, 0), (0, 0), (0, LANES - 3)))
        za = _to_head_rows(ua[:, 3 * C_CONV:], HEAD_DIM)
        ya = _sattn(page_table, qt, ck, cv, mm_s, csk, csv, ks, vs, cwk, cwv, kw, vw, gt, za, i)
        ya = ya.reshape(DEC_BATCH, N_HEADS, DEC_SEQ, HEAD_DIM).transpose(0, 2, 1, 3).reshape(N_SAMPLE, ATTN_W)
        h_s = _outproj(h_s, yc.astype(BF16), ya.astype(BF16), p_sample[i].reshape(N_SAMPLE, PLE_DIM), *out_w, final=final)
        hs = lambda x: heads(x, DEC_BATCH, DEC_SEQ)
        st_s.append((hs(kc), hs(vc), hs(ks), hs(vs),
                     jnp.concatenate([cache_win_k[i][:, DEC_SEQ:], hs(kw)], axis=1),
                     jnp.concatenate([cache_win_v[i][:, DEC_SEQ:], hs(vw)], axis=1), nc_s[:, HIST_OFF:]))

    y_prompt = h_p.reshape(1, SEQ, D_MODEL)
    y_sample = h_s.reshape(DEC_BATCH, DEC_SEQ, D_MODEL)
    outs_p = [jnp.stack([s[n] for s in st_p]) for n in range(7)]
    outs_s = [jnp.stack([s[n] for s in st_s]) for n in range(7)]
    return (y_prompt, y_sample, *outs_p, *outs_s)
```
